```python
import jax
import jax.numpy as jnp
from jax import lax
import numpy as np

D_MODEL = 1024
BATCH = 2
SEQ = 8192
DEPTH = 2

GRID_W = 64
CTX_LEN = 256
EPS = 1e-6
N_MOD = 6
A_GROUPS = 4
A_GROUP_DIM = 128
A_WIDTH = A_GROUPS * A_GROUP_DIM
CHUNK = 128
N_Q_HEADS = 4
N_KV_HEADS = 2
HEAD_DIM = 128
Q_BLOCK = 128
ROPE_THETA = 10000.0
B_WIDTH = N_Q_HEADS * HEAD_DIM
KV_WIDTH = N_KV_HEADS * HEAD_DIM
AB_IN = 2 * A_WIDTH + B_WIDTH + 2 * KV_WIDTH
AB_OUT = A_WIDTH + B_WIDTH
D_RNN = 1280
RNN_BLOCK = 128
N_RNN_BLOCKS = D_RNN // RNN_BLOCK
CONV_W = 4
RG_C = 8.0
PEER_HEADS = 8
N_KEYS = 128
N_EXPERTS = N_KEYS * N_KEYS
PEER_TOPK = 16
D_KEY = 256
TOKEN_BLOCK = 128
N_EVEN = (DEPTH + 1) // 2
N_ODD = DEPTH // 2

kernel_name = 'hybrid_gmlp_gqa_rglru_peer_trunk'


def rmsnorm(x, g):
    xf = x.astype(jnp.float32)
    y = xf * lax.rsqrt(jnp.mean(xf * xf, axis=-1, keepdims=True) + EPS)
    return y.astype(x.dtype) * g


def modulate(h, shift, scale):
    return h * (1 + scale) + shift


def axial_rope_tables(rows):
    n = rows * GRID_W
    row = jnp.broadcast_to(jnp.arange(rows, dtype=jnp.int32)[:, None], (rows, GRID_W)).reshape(n)
    col = jnp.broadcast_to(jnp.arange(GRID_W, dtype=jnp.int32)[None, :], (rows, GRID_W)).reshape(n)
    n_freq = HEAD_DIM // 4
    inv_freq = ROPE_THETA ** (-jnp.arange(n_freq, dtype=jnp.float32) / n_freq)
    ang = jnp.stack([row.astype(jnp.float32)[:, None] * inv_freq,
                     col.astype(jnp.float32)[:, None] * inv_freq], axis=1)
    return jnp.cos(ang), jnp.sin(ang)


def apply_axial_rope(x, cos, sin):
    b, n, h, d = x.shape
    xf = x.astype(jnp.float32).reshape(b, n, h, 2, 2, d // 4)
    c = cos[None, :, None]
    s = sin[None, :, None]
    x1, x2 = xf[..., 0, :], xf[..., 1, :]
    out = jnp.stack([x1 * c - x2 * s, x2 * c + x1 * s], axis=-2)
    return out.reshape(b, n, h, d).astype(x.dtype)


def block_attention(q, k, v):
    b, n, h, d = q.shape
    g = h // N_KV_HEADS
    qb = q.reshape(b, n // Q_BLOCK, Q_BLOCK, N_KV_HEADS, g, d).transpose(1, 0, 2, 3, 4, 5)
    scale = HEAD_DIM ** -0.5

    def one_block(qblk):
        s = jnp.einsum('bqkgd,bmkd->bkgqm', qblk, k).astype(jnp.float32) * scale
        p = jax.nn.softmax(s, axis=-1).astype(v.dtype)
        return jnp.einsum('bkgqm,bmkd->bqkgd', p, v)

    o = lax.map(one_block, qb)
    return o.transpose(1, 0, 2, 3, 4, 5).reshape(b, n, h * d)


def chunk_gmlp(za, norm_g, ws, bs):
    b, n, _ = za.shape
    za = jax.nn.gelu(za)
    u, vv = za[..., :A_WIDTH], za[..., A_WIDTH:]
    vv = rmsnorm(vv.reshape(b, n, A_GROUPS, A_GROUP_DIM), norm_g.reshape(A_GROUPS, A_GROUP_DIM))
    vv = vv.reshape(b, n // CHUNK, CHUNK, A_GROUPS, A_GROUP_DIM)
    s = jnp.einsum('gpq,bnqgc->bnpgc', ws, vv) + bs.T[None, None, :, :, None]
    return u * s.reshape(b, n, A_WIDTH)


def split_ab(z):
    b, n, _ = z.shape
    o1 = 2 * A_WIDTH
    o2 = o1 + B_WIDTH
    o3 = o2 + KV_WIDTH
    za = z[..., :o1]
    q = z[..., o1:o2].reshape(b, n, N_Q_HEADS, HEAD_DIM)
    k = z[..., o2:o3].reshape(b, n, N_KV_HEADS, HEAD_DIM)
    v = z[..., o3:].reshape(b, n, N_KV_HEADS, HEAD_DIM)
    return za, q, k, v


def mixer_ab(h, hc, cos, sin, w_in, gm_g, ws, bs, qg, kg, w_out, need_ctx):
    za, q, k, v = split_ab(h @ w_in)
    zac, qc, kc, vc = split_ab(hc @ w_in)
    q = apply_axial_rope(rmsnorm(q, qg), cos, sin)
    k = apply_axial_rope(rmsnorm(k, kg), cos, sin)
    kc = rmsnorm(kc, kg)
    k_all = jnp.concatenate([kc, k], axis=1)
    v_all = jnp.concatenate([vc, v], axis=1)
    y = jnp.concatenate([chunk_gmlp(za, gm_g, ws, bs), block_attention(q, k_all, v_all)], axis=-1) @ w_out
    if not need_ctx:
        return y, None
    qc = rmsnorm(qc, qg)
    yc = jnp.concatenate([chunk_gmlp(zac, gm_g, ws, bs), block_attention(qc, kc, vc)], axis=-1) @ w_out
    return y, yc


def centred_dwconv(x, w, b):
    y = lax.conv_general_dilated(
        x, w[:, None, :], window_strides=(1,),
        padding=[(CONV_W // 2, CONV_W - 1 - CONV_W // 2)],
        dimension_numbers=('NWC', 'WIO', 'NWC'),
        feature_group_count=x.shape[-1])
    return y + b


def linear_combine(e1, e2):
    a1, b1 = e1
    a2, b2 = e2
    return a1 * a2, a2 * b1 + b2


def rglru_coeffs(xr, wa, ba, wx, bx, lam):
    b, n, _ = xr.shape
    xb = xr.reshape(b, n, N_RNN_BLOCKS, RNN_BLOCK)
    r = jax.nn.sigmoid(jnp.einsum('bnhi,hij->bnhj', xb, wa).reshape(b, n, D_RNN) + ba)
    i = jax.nn.sigmoid(jnp.einsum('bnhi,hij->bnhj', xb, wx).reshape(b, n, D_RNN) + bx)
    log_a = -RG_C * r.astype(jnp.float32) * jax.nn.softplus(-lam.astype(jnp.float32))
    u = jnp.sqrt(-jnp.expm1(2.0 * log_a)) * (i * xr).astype(jnp.float32)
    return jnp.exp(log_a), u


def mixer_rglru(h, hc, w_in, conv_w, conv_b, wa, ba, wx, bx, lam, w_out, need_ctx):
    z = h @ w_in
    zc = hc @ w_in
    gate, xr = z[..., :D_RNN], z[..., D_RNN:]
    gate_c, xr_c = zc[..., :D_RNN], zc[..., D_RNN:]
    xr = centred_dwconv(xr, conv_w, conv_b)
    xr_c = centred_dwconv(xr_c, conv_w, conv_b)
    lat_states = []
    ctx_states = []
    for d in range(2):
        rev = d == 1
        ac, uc = rglru_coeffs(xr_c, wa[d], ba[d], wx[d], bx[d], lam[d])
        _, hc_seq = lax.associative_scan(linear_combine, (ac, uc), reverse=rev, axis=1)
        h_end = hc_seq[:, 0] if rev else hc_seq[:, -1]
        al, ul = rglru_coeffs(xr, wa[d], ba[d], wx[d], bx[d], lam[d])
        a_pre, h_seq = lax.associative_scan(linear_combine, (al, ul), reverse=rev, axis=1)
        lat_states.append(h_seq + a_pre * h_end[:, None, :])
        ctx_states.append(hc_seq)
    y_rec = (lat_states[0] + lat_states[1]).astype(h.dtype)
    y = (jax.nn.gelu(gate) * y_rec) @ w_out
    if not need_ctx:
        return y, None
    yc_rec = (ctx_states[0] + ctx_states[1]).astype(hc.dtype)
    yc = (jax.nn.gelu(gate_c) * yc_rec) @ w_out
    return y, yc


def peer_ffn(h, wq, keys, u_tab, v_tab):
    b, n, d = h.shape
    t = h.reshape(-1, TOKEN_BLOCK, d)

    def one_block(tb):
        q = (tb @ wq).reshape(TOKEN_BLOCK, PEER_HEADS, 2, D_KEY // 2)
        s = jnp.einsum('thsd,hskd->thsk', q, keys)
        v_half, i_half = lax.top_k(s, PEER_TOPK)
        cand = v_half[:, :, 0, :, None] + v_half[:, :, 1, None, :]
        cand_idx = i_half[:, :, 0, :, None] * N_KEYS + i_half[:, :, 1, None, :]
        top_s, pos = lax.top_k(cand.reshape(TOKEN_BLOCK, PEER_HEADS, PEER_TOPK * PEER_TOPK), PEER_TOPK)
        eidx = jnp.take_along_axis(cand_idx.reshape(TOKEN_BLOCK, PEER_HEADS, PEER_TOPK * PEER_TOPK), pos, axis=-1)
        gate = jax.nn.softmax(top_s.astype(jnp.float32), axis=-1).astype(tb.dtype)
        eidx = eidx.reshape(TOKEN_BLOCK, PEER_HEADS * PEER_TOPK)
        gate = gate.reshape(TOKEN_BLOCK, PEER_HEADS * PEER_TOPK)
        ue = jnp.take(u_tab, eidx, axis=0)
        ve = jnp.take(v_tab, eidx, axis=0)
        act = jax.nn.gelu(jnp.einsum('td,ted->te', tb, ue)) * gate
        return jnp.einsum('te,ted->td', act, ve)

    return lax.map(one_block, t).reshape(b, n, d)


def setup_inputs(seed: int = 0) -> dict:
    key = jax.random.key(seed)
    ks = jax.random.split(key, 32)
    D = D_MODEL

    def nrm(k, shape, scale):
        return jax.random.normal(k, shape, jnp.float32) * scale

    def gain(k, shape):
        return 1.0 + nrm(k, shape, 0.05)

    a0 = jax.random.uniform(ks[22], (N_ODD, 2, D_RNN), jnp.float32, minval=0.9, maxval=0.999)
    a_base = a0 ** (1.0 / RG_C)
    rg_lambda = jnp.log(a_base) - jnp.log1p(-a_base)
    return {
        'x': nrm(ks[0], (BATCH, SEQ, D), 1.0),
        'c': nrm(ks[1], (BATCH, D), 1.0),
        'ctx': nrm(ks[2], (BATCH, CTX_LEN, D), 1.0),
        'c_ctx': nrm(ks[3], (D,), 1.0),
        'mod_w': nrm(ks[4], (DEPTH, D, N_MOD * D), 0.5 * D ** -0.5),
        'mod_b': nrm(ks[5], (DEPTH, N_MOD * D), 0.02),
        'norm1_g': gain(ks[6], (DEPTH, D)),
        'norm2_g': gain(ks[7], (DEPTH, D)),
        'ab_w_in': nrm(ks[8], (N_EVEN, D, AB_IN), D ** -0.5),
        'gmlp_norm_g': gain(ks[9], (N_EVEN, A_WIDTH)),
        'gmlp_ws': nrm(ks[10], (N_EVEN, A_GROUPS, CHUNK, CHUNK), CHUNK ** -0.5),
        'gmlp_bs': gain(ks[11], (N_EVEN, A_GROUPS, CHUNK)),
        'q_norm_g': gain(ks[12], (N_EVEN, HEAD_DIM)),
        'k_norm_g': gain(ks[13], (N_EVEN, HEAD_DIM)),
        'ab_w_out': nrm(ks[14], (N_EVEN, AB_OUT, D), AB_OUT ** -0.5),
        'rg_w_in': nrm(ks[15], (N_ODD, D, 2 * D_RNN), D ** -0.5),
        'rg_conv_w': nrm(ks[16], (N_ODD, CONV_W, D_RNN), CONV_W ** -0.5),
        'rg_conv_b': nrm(ks[17], (N_ODD, D_RNN), 0.02),
        'rg_wa': nrm(ks[18], (N_ODD, 2, N_RNN_BLOCKS, RNN_BLOCK, RNN_BLOCK), RNN_BLOCK ** -0.5),
        'rg_ba': nrm(ks[19], (N_ODD, 2, D_RNN), 0.02),
        'rg_wx': nrm(ks[20], (N_ODD, 2, N_RNN_BLOCKS, RNN_BLOCK, RNN_BLOCK), RNN_BLOCK ** -0.5),
        'rg_bx': nrm(ks[21], (N_ODD, 2, D_RNN), 0.02),
        'rg_lambda': rg_lambda,
        'rg_w_out': nrm(ks[23], (N_ODD, D_RNN, D), D_RNN ** -0.5),
        'peer_wq': nrm(ks[24], (DEPTH, D, PEER_HEADS * D_KEY), D ** -0.5),
        'peer_keys': nrm(ks[25], (DEPTH, PEER_HEADS, 2, N_KEYS, D_KEY // 2), (D_KEY // 2) ** -0.5),
        'peer_u': nrm(ks[26], (DEPTH, N_EXPERTS, D), D ** -0.5),
        'peer_v': nrm(ks[27], (DEPTH, N_EXPERTS, D), 1.0),
        'final_norm_g': gain(ks[28], (D,)),
    }


def reference(x, c, ctx, c_ctx, mod_w, mod_b, norm1_g, norm2_g, ab_w_in, gmlp_norm_g, gmlp_ws, gmlp_bs,
              q_norm_g, k_norm_g, ab_w_out, rg_w_in, rg_conv_w, rg_conv_b, rg_wa, rg_ba, rg_wx, rg_bx,
              rg_lambda, rg_w_out, peer_wq, peer_keys, peer_u, peer_v, final_norm_g):
    n = x.shape[1]
    rows = n // GRID_W
    cos, sin = axial_rope_tables(rows)
    silu_c = jax.nn.silu(c)
    silu_cc = jax.nn.silu(c_ctx)
    for layer in range(DEPTH):
        last = layer == DEPTH - 1
        j = layer // 2
        mod = (silu_c @ mod_w[layer] + mod_b[layer])[:, None, :]
        modc = silu_cc @ mod_w[layer] + mod_b[layer]
        sh1, sc1, g1, sh2, sc2, g2 = jnp.split(mod, N_MOD, axis=-1)
        sh1c, sc1c, g1c, sh2c, sc2c, g2c = jnp.split(modc, N_MOD, axis=-1)
        h = modulate(rmsnorm(x, norm1_g[layer]), sh1, sc1)
        hc = modulate(rmsnorm(ctx, norm1_g[layer]), sh1c, sc1c)
        if layer % 2 == 0:
            y, yc = mixer_ab(h, hc, cos, sin, ab_w_in[j], gmlp_norm_g[j], gmlp_ws[j], gmlp_bs[j],
                             q_norm_g[j], k_norm_g[j], ab_w_out[j], not last)
        else:
            y, yc = mixer_rglru(h, hc, rg_w_in[j], rg_conv_w[j], rg_conv_b[j], rg_wa[j], rg_ba[j],
                                rg_wx[j], rg_bx[j], rg_lambda[j], rg_w_out[j], not last)
        x = x + g1 * y
        x = x + g2 * peer_ffn(modulate(rmsnorm(x, norm2_g[layer]), sh2, sc2),
                              peer_wq[layer], peer_keys[layer], peer_u[layer], peer_v[layer])
        if not last:
            ctx = ctx + g1c * yc
            ctx = ctx + g2c * peer_ffn(modulate(rmsnorm(ctx, norm2_g[layer]), sh2c, sc2c),
                                       peer_wq[layer], peer_keys[layer], peer_u[layer], peer_v[layer])
    return rmsnorm(x, final_norm_g)
```

```python
import functools

import jax
import jax.numpy as jnp
from jax import lax
from jax.experimental import pallas as pl
from jax.experimental.pallas import tpu as pltpu

f32 = jnp.float32
bf16 = jnp.bfloat16

D_MODEL = 1024
BATCH = 2
SEQ = 8192
DEPTH = 2
GRID_W = 64
CTX_LEN = 256
EPS = 1e-6
N_MOD = 6
A_GROUPS = 4
A_WIDTH = 512
CHUNK = 128
N_Q_HEADS = 4
N_KV_HEADS = 2
HEAD_DIM = 128
ROPE_THETA = 10000.0
B_WIDTH = N_Q_HEADS * HEAD_DIM
KV_WIDTH = N_KV_HEADS * HEAD_DIM
AB_IN = 2 * A_WIDTH + B_WIDTH + 2 * KV_WIDTH
D_RNN = 1280
RNN_BLOCK = 128
N_RNN_BLOCKS = D_RNN // RNN_BLOCK
CONV_W = 4
RG_C = 8.0
PEER_HEADS = 8
N_KEYS = 128
N_EXPERTS = N_KEYS * N_KEYS
PEER_TOPK = 16
D_KEY = 256

N_LAT = BATCH * SEQ
N_TOK = N_LAT + BATCH * CTX_LEN
TB = 256
NB_LAT = N_LAT // TB
NB_ALL = N_TOK // TB
NB_SEQ = SEQ // TB
ATT_CK = 1024
PEER_TT = 512
PEER_EB = 1024
SCAN_R = 256
VMEM_LIMIT = 56 * 1024 * 1024

MOD_SH1, MOD_SC1, MOD_G1, MOD_SH2, MOD_SC2, MOD_G2 = range(6)

_NT = (((1,), (1,)), ((), ()))


def _rms(x, g):
    return x * lax.rsqrt(jnp.mean(x * x, axis=-1, keepdims=True) + EPS) * g


def _gelu(x):
    return 0.5 * x * (1.0 + jnp.tanh(0.7978845608028654 * (x + 0.044715 * (x * x * x))))


def _mod_row_of_block(j):
    return jnp.where(j < NB_LAT, j // NB_SEQ, 2)


def _mrow(mod_ref, k):
    return mod_ref[0, k:k + 1, :]


def _mod_kernel(c_ref, w_ref, b_ref, o_ref):
    c = c_ref[...]
    s = c * jax.nn.sigmoid(c)
    o_ref[0] = jnp.dot(s, w_ref[0], precision=lax.Precision.HIGHEST,
                       preferred_element_type=f32) + b_ref[0]


def _mod_vectors(cvec, mod_w, mod_b):
    nb = 1536
    width = N_MOD * D_MODEL
    out = pl.pallas_call(
        _mod_kernel,
        grid=(DEPTH, width // nb),
        in_specs=[pl.BlockSpec((8, D_MODEL), lambda l, n: (0, 0)),
                  pl.BlockSpec((1, D_MODEL, nb), lambda l, n: (l, 0, n)),
                  pl.BlockSpec((1, 1, nb), lambda l, n: (l, 0, n))],
        out_specs=pl.BlockSpec((1, 8, nb), lambda l, n: (l, 0, n)),
        out_shape=jax.ShapeDtypeStruct((DEPTH, 8, width), f32),
        name="mod_vectors",
    )(cvec, mod_w, mod_b.reshape(DEPTH, 1, width))
    mods = out.reshape(DEPTH, 8, N_MOD, D_MODEL)[:, :3]
    return jnp.pad(mods, ((0, 0), (0, 0), (0, 8 - N_MOD), (0, 0)))


def _ab_in_kernel(x_ref, mod_ref, n1_ref, win_ref, gmg_ref, ws_ref, gb_ref, qg_ref, kg_ref,
                  cos_ref, sin_ref, ya_ref, q_ref, k_ref, v_ref, z_sc):
    h = _rms(x_ref[...], n1_ref[...]) * (1.0 + _mrow(mod_ref, MOD_SC1)) + _mrow(mod_ref, MOD_SH1)
    z_sc[...] = jnp.dot(h.astype(bf16), win_ref[...], preferred_element_type=f32)

    cosf = cos_ref[...]
    sinf = sin_ref[...]
    lane = lax.broadcasted_iota(jnp.int32, (TB, HEAD_DIM), 1)
    low_part = (lane & 32) == 0

    def rope(t):
        partner = jnp.where(low_part, pltpu.roll(t, HEAD_DIM - 32, 1), pltpu.roll(t, 32, 1))
        return t * cosf + partner * sinf

    for g in range(A_GROUPS):
        lo = g * CHUNK
        u = _gelu(z_sc[:, lo:lo + CHUNK])
        vg = _gelu(z_sc[:, A_WIDTH + lo:A_WIDTH + lo + CHUNK])
        vn = _rms(vg, gmg_ref[:, lo:lo + CHUNK]).astype(bf16)
        for c in range(TB // CHUNK):
            r0 = c * CHUNK
            s = jnp.dot(ws_ref[g], vn[r0:r0 + CHUNK], preferred_element_type=f32) + gb_ref[g]
            ya_ref[r0:r0 + CHUNK, lo:lo + CHUNK] = (u[r0:r0 + CHUNK] * s).astype(bf16)

    q0 = 2 * A_WIDTH
    for hq in range(N_Q_HEADS):
        lo = hq * HEAD_DIM
        qh = _rms(z_sc[:, q0 + lo:q0 + lo + HEAD_DIM], qg_ref[...])
        q_ref[:, lo:lo + HEAD_DIM] = rope(qh).astype(bf16)
    k0 = q0 + B_WIDTH
    for hk in range(N_KV_HEADS):
        lo = hk * HEAD_DIM
        kh = _rms(z_sc[:, k0 + lo:k0 + lo + HEAD_DIM], kg_ref[...])
        k_ref[:, lo:lo + HEAD_DIM] = rope(kh).astype(bf16)
    v0 = k0 + KV_WIDTH
    v_ref[...] = z_sc[:, v0:v0 + KV_WIDTH].astype(bf16)


def _ab_in(xs, mods, n1, w_in, gm_g, ws, gbias, qg, kg, cosf, sinf):
    full = lambda shape: pl.BlockSpec(shape, lambda j: (0,) * len(shape))
    rope_blk = lambda j: (jnp.where(j < NB_LAT, j % NB_SEQ, NB_SEQ), 0)
    return pl.pallas_call(
        _ab_in_kernel,
        grid=(NB_ALL,),
        in_specs=[pl.BlockSpec((TB, D_MODEL), lambda j: (j, 0)),
                  pl.BlockSpec((1, 8, D_MODEL), lambda j: (_mod_row_of_block(j), 0, 0)),
                  full((1, D_MODEL)), full((D_MODEL, AB_IN)), full((1, A_WIDTH)),
                  full((A_GROUPS, CHUNK, CHUNK)), full((A_GROUPS, CHUNK, CHUNK)),
                  full((1, HEAD_DIM)), full((1, HEAD_DIM)),
                  pl.BlockSpec((TB, HEAD_DIM), rope_blk), pl.BlockSpec((TB, HEAD_DIM), rope_blk)],
        out_specs=[pl.BlockSpec((TB, A_WIDTH), lambda j: (j, 0)),
                   pl.BlockSpec((TB, B_WIDTH), lambda j: (j, 0)),
                   pl.BlockSpec((TB, KV_WIDTH), lambda j: (j, 0)),
                   pl.BlockSpec((TB, KV_WIDTH), lambda j: (j, 0))],
        out_shape=[jax.ShapeDtypeStruct((N_TOK, A_WIDTH), bf16),
                   jax.ShapeDtypeStruct((N_TOK, B_WIDTH), bf16),
                   jax.ShapeDtypeStruct((N_TOK, KV_WIDTH), bf16),
                   jax.ShapeDtypeStruct((N_TOK, KV_WIDTH), bf16)],
        scratch_shapes=[pltpu.VMEM((TB, AB_IN), f32)],
        compiler_params=pltpu.CompilerParams(dimension_semantics=("parallel",),
                                             vmem_limit_bytes=VMEM_LIMIT),
        name="ab_in_proj",
    )(xs, mods, n1, w_in, gm_g, ws, gbias, qg, kg, cosf, sinf)


def _attn_kernel(q_ref, kl_ref, kc_ref, vl_ref, vc_ref, o_ref, m_sc, l_sc, acc_sc):
    j = pl.program_id(0)
    scale = HEAD_DIM ** -0.5
    group = N_Q_HEADS // N_KV_HEADS
    for kh in range(N_KV_HEADS):
        klo = kh * HEAD_DIM
        q2 = jnp.concatenate(
            [q_ref[:, (kh * group + gi) * HEAD_DIM:(kh * group + gi + 1) * HEAD_DIM]
             for gi in range(group)], axis=0)

        s = lax.dot_general(q2, kc_ref[:, klo:klo + HEAD_DIM], _NT,
                            preferred_element_type=f32) * scale
        m = jnp.max(s, axis=-1, keepdims=True)
        p = jnp.exp(s - m)
        m_sc[...] = m
        l_sc[...] = jnp.sum(p, axis=-1, keepdims=True)
        acc_sc[...] = jnp.dot(p.astype(bf16), vc_ref[:, klo:klo + HEAD_DIM],
                              preferred_element_type=f32)

        @pl.when(j < NB_LAT)
        def _():
            def step(i, carry):
                r0 = pl.multiple_of(i * ATT_CK, ATT_CK)
                s = lax.dot_general(q2, kl_ref[pl.ds(r0, ATT_CK), klo:klo + HEAD_DIM], _NT,
                                    preferred_element_type=f32) * scale
                m_old = m_sc[...]
                m_new = jnp.maximum(m_old, jnp.max(s, axis=-1, keepdims=True))
                alpha = jnp.exp(m_old - m_new)
                p = jnp.exp(s - m_new)
                l_sc[...] = alpha * l_sc[...] + jnp.sum(p, axis=-1, keepdims=True)
                acc_sc[...] = alpha * acc_sc[...] + jnp.dot(
                    p.astype(bf16), vl_ref[pl.ds(r0, ATT_CK), klo:klo + HEAD_DIM],
                    preferred_element_type=f32)
                m_sc[...] = m_new
                return carry
            lax.fori_loop(0, SEQ // ATT_CK, step, 0)

        o = acc_sc[...] / l_sc[...]
        for gi in range(group):
            lo = (kh * group + gi) * HEAD_DIM
            o_ref[:, lo:lo + HEAD_DIM] = o[gi * TB:(gi + 1) * TB].astype(bf16)


def _attention(q, k, v):
    batch_of = lambda j: jnp.where(j < NB_LAT, j // NB_SEQ, j - NB_LAT)
    lat_spec = pl.BlockSpec((SEQ, KV_WIDTH), lambda j: (batch_of(j), 0))
    ctx_spec = pl.BlockSpec((CTX_LEN, KV_WIDTH), lambda j: (N_LAT // CTX_LEN + batch_of(j), 0))
    rows = (N_Q_HEADS // N_KV_HEADS) * TB
    return pl.pallas_call(
        _attn_kernel,
        grid=(NB_ALL,),
        in_specs=[pl.BlockSpec((TB, B_WIDTH), lambda j: (j, 0)), lat_spec, ctx_spec, lat_spec, ctx_spec],
        out_specs=pl.BlockSpec((TB, B_WIDTH), lambda j: (j, 0)),
        out_shape=jax.ShapeDtypeStruct((N_TOK, B_WIDTH), bf16),
        scratch_shapes=[pltpu.VMEM((rows, 1), f32), pltpu.VMEM((rows, 1), f32),
                        pltpu.VMEM((rows, HEAD_DIM), f32)],
        compiler_params=pltpu.CompilerParams(dimension_semantics=("parallel",),
                                             vmem_limit_bytes=VMEM_LIMIT),
        name="gqa_attention",
    )(q, k, k, v, v)


def _out_proj_kernel(*refs, n_in):
    a_refs = refs[:n_in]
    w_refs = refs[n_in:2 * n_in]
    x_ref, mod_ref, n2_ref, x1_ref, h2_ref = refs[2 * n_in:]
    y = jnp.dot(a_refs[0][...], w_refs[0][...], preferred_element_type=f32)
    for a_ref, w_ref in zip(a_refs[1:], w_refs[1:]):
        y = y + jnp.dot(a_ref[...], w_ref[...], preferred_element_type=f32)
    x1 = x_ref[...] + _mrow(mod_ref, MOD_G1) * y
    x1_ref[...] = x1
    h2 = _rms(x1, n2_ref[...]) * (1.0 + _mrow(mod_ref, MOD_SC2)) + _mrow(mod_ref, MOD_SH2)
    h2_ref[...] = h2.astype(bf16)


def _out_proj(acts, weights, xs, mods, n2, n_blocks):
    n_rows = n_blocks * TB
    in_specs = [pl.BlockSpec((TB, a.shape[1]), lambda j: (j, 0)) for a in acts]
    in_specs += [pl.BlockSpec(w.shape, lambda j: (0, 0)) for w in weights]
    in_specs += [pl.BlockSpec((TB, D_MODEL), lambda j: (j, 0)),
                 pl.BlockSpec((1, 8, D_MODEL), lambda j: (_mod_row_of_block(j), 0, 0)),
                 pl.BlockSpec((1, D_MODEL), lambda j: (0, 0))]
    return pl.pallas_call(
        functools.partial(_out_proj_kernel, n_in=len(acts)),
        grid=(n_blocks,),
        in_specs=in_specs,
        out_specs=[pl.BlockSpec((TB, D_MODEL), lambda j: (j, 0)),
                   pl.BlockSpec((TB, D_MODEL), lambda j: (j, 0))],
        out_shape=[jax.ShapeDtypeStruct((n_rows, D_MODEL), f32),
                   jax.ShapeDtypeStruct((n_rows, D_MODEL), bf16)],
        compiler_params=pltpu.CompilerParams(dimension_semantics=("parallel",),
                                             vmem_limit_bytes=VMEM_LIMIT),
        name="mixer_out_proj",
    )(*acts, *weights, xs, mods, n2)


def _route_kernel(h_ref, wqt_ref, keys_ref, c0_ref, e0_ref, r1_ref, e1_ref, q_sc, s_sc, a_sc, r_sc):
    neg = -jnp.inf
    q_sc[...] = lax.dot_general(wqt_ref[...], h_ref[...], _NT,
                                preferred_element_type=f32).astype(bf16)

    def half_topk(hh, carry):
        r0 = pl.multiple_of(hh * N_KEYS, N_KEYS)
        s = jnp.dot(keys_ref[hh], q_sc[pl.ds(r0, N_KEYS), :], preferred_element_type=f32)
        s_sc[hh] = s
        rank = jnp.full((N_KEYS, TB), float(PEER_TOPK), f32)
        tops = []
        for k in range(PEER_TOPK):
            m = jnp.max(s, axis=0, keepdims=True)
            hit = s == m
            rank = jnp.where(hit, float(k), rank)
            s = jnp.where(hit, neg, s)
            tops.append(m)
        r_sc[hh] = rank
        a_sc[hh] = jnp.concatenate(tops, axis=0)
        return carry

    lax.fori_loop(0, 2 * PEER_HEADS, half_topk, 0)

    def head_gate(h, carry):
        a0 = a_sc[2 * h]
        a1 = a_sc[2 * h + 1]
        cands = [a0[k:k + 1, :] + a1 for k in range(PEER_TOPK)]
        work = jnp.concatenate(cands, axis=0)
        tau = None
        for _ in range(PEER_TOPK):
            tau = jnp.max(work, axis=0, keepdims=True)
            work = jnp.where(work == tau, neg, work)
        top = a0[0:1, :] + a1[0:1, :]
        rank0 = r_sc[2 * h]
        rank1 = r_sc[2 * h + 1]
        z = jnp.zeros((1, TB), f32)
        count0 = jnp.zeros((N_KEYS, TB), f32)
        for k in range(PEER_TOPK):
            sel = cands[k] >= tau
            z = z + jnp.sum(jnp.where(sel, jnp.exp(cands[k] - top), 0.0), axis=0, keepdims=True)
            cnt = jnp.sum(jnp.where(sel, 1.0, 0.0), axis=0, keepdims=True)
            count0 = jnp.where(rank0 == float(k), cnt, count0)
        c0_ref[h] = count0
        r1_ref[h] = rank1
        e0_ref[h] = jnp.where(rank0 < float(PEER_TOPK), jnp.exp(s_sc[2 * h] - a0[0:1, :]), 0.0) / z
        e1_ref[h] = jnp.where(rank1 < float(PEER_TOPK), jnp.exp(s_sc[2 * h + 1] - a1[0:1, :]), 0.0)
        return carry

    lax.fori_loop(0, PEER_HEADS, head_gate, 0)


def _peer_route(h2, wq_t, keys):
    n_rows = h2.shape[0]
    tab = jax.ShapeDtypeStruct((PEER_HEADS, N_KEYS, n_rows), f32)
    tab_spec = pl.BlockSpec((PEER_HEADS, N_KEYS, TB), lambda j: (0, 0, j))
    return pl.pallas_call(
        _route_kernel,
        grid=(n_rows // TB,),
        in_specs=[pl.BlockSpec((TB, D_MODEL), lambda j: (j, 0)),
                  pl.BlockSpec(wq_t.shape, lambda j: (0, 0)),
                  pl.BlockSpec(keys.shape, lambda j: (0, 0, 0))],
        out_specs=[tab_spec] * 4,
        out_shape=[tab] * 4,
        scratch_shapes=[pltpu.VMEM((2 * PEER_HEADS * N_KEYS, TB), bf16),
                        pltpu.VMEM((2 * PEER_HEADS, N_KEYS, TB), f32),
                        pltpu.VMEM((2 * PEER_HEADS, PEER_TOPK, TB), f32),
                        pltpu.VMEM((2 * PEER_HEADS, N_KEYS, TB), f32)],
        compiler_params=pltpu.CompilerParams(dimension_semantics=("parallel",),
                                             vmem_limit_bytes=VMEM_LIMIT),
        name="peer_route",
    )(h2, wq_t, keys)


def _peer_dense_kernel(h_ref, u_ref, vt_ref, c0_ref, e0_ref, r1_ref, e1_ref, x1_ref, mod_ref, fg_ref,
                       o_ref, acc_sc, sc_sc, act_sc, *, final_norm):
    eb = pl.program_id(1)

    @pl.when(eb == 0)
    def _():
        acc_sc[...] = jnp.zeros_like(acc_sc)

    sc_sc[...] = lax.dot_general(u_ref[...], h_ref[...], _NT, preferred_element_type=f32)
    for ii in range(PEER_EB // N_KEYS):
        lo = ii * N_KEYS
        gate = jnp.zeros((N_KEYS, PEER_TT), f32)
        for h in range(PEER_HEADS):
            picked = r1_ref[h] < c0_ref[h, ii:ii + 1, :]
            gate = gate + jnp.where(picked, e1_ref[h], 0.0) * e0_ref[h, ii:ii + 1, :]
        act_sc[lo:lo + N_KEYS, :] = (_gelu(sc_sc[lo:lo + N_KEYS, :]) * gate).astype(bf16)
    acc_sc[...] += jnp.dot(vt_ref[...], act_sc[...], preferred_element_type=f32)

    @pl.when(eb == pl.num_programs(1) - 1)
    def _():
        x2 = x1_ref[...] + _mrow(mod_ref, MOD_G2) * acc_sc[...].T
        if final_norm:
            x2 = _rms(x2, fg_ref[...])
        o_ref[...] = x2


def _peer_dense(h2, tabs, u_tab, v_tab_t, x1, mods, final_g, final_norm):
    n_rows = h2.shape[0]
    c0, e0, r1, e1 = tabs
    tiles_per_seq = SEQ // PEER_TT
    mod_row = lambda t, e: (jnp.where(t < N_LAT // PEER_TT, t // tiles_per_seq, 2), 0, 0)
    row_spec = pl.BlockSpec((PEER_HEADS, PEER_EB // N_KEYS, PEER_TT), lambda t, e: (0, e, t))
    key_spec = pl.BlockSpec((PEER_HEADS, N_KEYS, PEER_TT), lambda t, e: (0, 0, t))
    return pl.pallas_call(
        functools.partial(_peer_dense_kernel, final_norm=final_norm),
        grid=(n_rows // PEER_TT, N_EXPERTS // PEER_EB),
        in_specs=[pl.BlockSpec((PEER_TT, D_MODEL), lambda t, e: (t, 0)),
                  pl.BlockSpec((PEER_EB, D_MODEL), lambda t, e: (e, 0)),
                  pl.BlockSpec((D_MODEL, PEER_EB), lambda t, e: (0, e)),
                  row_spec, row_spec, key_spec, key_spec,
                  pl.BlockSpec((PEER_TT, D_MODEL), lambda t, e: (t, 0)),
                  pl.BlockSpec((1, 8, D_MODEL), mod_row),
                  pl.BlockSpec((1, D_MODEL), lambda t, e: (0, 0))],
        out_specs=pl.BlockSpec((PEER_TT, D_MODEL), lambda t, e: (t, 0)),
        out_shape=jax.ShapeDtypeStruct((n_rows, D_MODEL), f32),
        scratch_shapes=[pltpu.VMEM((D_MODEL, PEER_TT), f32),
                        pltpu.VMEM((PEER_EB, PEER_TT), f32),
                        pltpu.VMEM((PEER_EB, PEER_TT), bf16)],
        compiler_params=pltpu.CompilerParams(dimension_semantics=("parallel", "arbitrary"),
                                             vmem_limit_bytes=VMEM_LIMIT),
        name="peer_dense",
    )(h2, u_tab, v_tab_t, c0, e0, r1, e1, x1, mods, final_g)


def _rg_in_kernel(x_ref, mod_ref, n1_ref, win_ref, gg_ref, xr_ref):
    h = _rms(x_ref[...], n1_ref[...]) * (1.0 + _mrow(mod_ref, MOD_SC1)) + _mrow(mod_ref, MOD_SH1)
    z = jnp.dot(h.astype(bf16), win_ref[...], preferred_element_type=f32)
    gg_ref[...] = _gelu(z[:, :D_RNN]).astype(bf16)
    xr_ref[...] = z[:, D_RNN:]


def _rg_in(xs, mods, n1, w_in):
    return pl.pallas_call(
        _rg_in_kernel,
        grid=(NB_ALL,),
        in_specs=[pl.BlockSpec((TB, D_MODEL), lambda j: (j, 0)),
                  pl.BlockSpec((1, 8, D_MODEL), lambda j: (_mod_row_of_block(j), 0, 0)),
                  pl.BlockSpec((1, D_MODEL), lambda j: (0, 0)),
                  pl.BlockSpec((D_MODEL, 2 * D_RNN), lambda j: (0, 0))],
        out_specs=[pl.BlockSpec((TB, D_RNN), lambda j: (j, 0)),
                   pl.BlockSpec((TB, D_RNN), lambda j: (j, 0))],
        out_shape=[jax.ShapeDtypeStruct((N_TOK, D_RNN), bf16),
                   jax.ShapeDtypeStruct((N_TOK, D_RNN), f32)],
        compiler_params=pltpu.CompilerParams(dimension_semantics=("parallel",),
                                             vmem_limit_bytes=VMEM_LIMIT),
        name="rglru_in_proj",
    )(xs, mods, n1, w_in)


def _rglru_kernel(xl_ref, xc_ref, gg_ref, cw_ref, cb_ref, wa_ref, wx_ref, ba_ref, bx_ref, lam_ref,
                  y_ref, xpad_sc, cpad_sc, hf_sc):
    R = SCAN_R
    n_tiles = SEQ // R
    pad = 8
    zeros = jnp.zeros((pad, RNN_BLOCK), f32)
    xpad_sc[0:pad] = zeros
    xpad_sc[pad + SEQ:2 * pad + SEQ] = zeros
    cpad_sc[0:pad] = zeros
    cpad_sc[pad + CTX_LEN:2 * pad + CTX_LEN] = zeros
    cpad_sc[pad:pad + CTX_LEN] = xc_ref[...]

    def copy_in(t, carry):
        r0 = pl.multiple_of(t * R, R)
        xpad_sc[pl.ds(r0 + pad, R)] = xl_ref[pl.ds(r0, R)]
        return carry
    lax.fori_loop(0, n_tiles, copy_in, 0)

    cw = cw_ref[...]
    cb = cb_ref[...]

    def conv(win):
        acc = cb
        for w in range(CONV_W):
            lo = pad - CONV_W // 2 + w
            acc = acc + win[lo:lo + R] * cw[w:w + 1]
        return acc

    def coeffs(xc, d):
        xb = xc.astype(bf16)
        r = jax.nn.sigmoid(jnp.dot(xb, wa_ref[d, 0], preferred_element_type=f32) + ba_ref[d:d + 1, :])
        i = jax.nn.sigmoid(jnp.dot(xb, wx_ref[d, 0], preferred_element_type=f32) + bx_ref[d:d + 1, :])
        nl = -lam_ref[d:d + 1, :]
        softplus = jnp.maximum(nl, 0.0) + jnp.log1p(jnp.exp(-jnp.abs(nl)))
        log_a = (-RG_C) * r * softplus
        a = jnp.exp(log_a)
        one_minus_a2 = -jnp.tanh(log_a) * (a * a + 1.0)
        return a, jnp.sqrt(one_minus_a2) * (i * xc)

    row = lax.broadcasted_iota(jnp.int32, (R, RNN_BLOCK), 0)

    def scan(a, u, reverse):
        s = 1
        while s < R:
            if reverse:
                edge = row >= R - s
                a_prev = pltpu.roll(a, R - s, 0)
                u_prev = pltpu.roll(u, R - s, 0)
            else:
                edge = row < s
                a_prev = pltpu.roll(a, s, 0)
                u_prev = pltpu.roll(u, s, 0)
            u = a * jnp.where(edge, 0.0, u_prev) + u
            a = a * jnp.where(edge, 1.0, a_prev)
            s *= 2
        return a, u

    xcc = conv(cpad_sc[...])
    _, hc = scan(*coeffs(xcc, 0), reverse=False)
    end_f = hc[R - 1:R]
    _, hc = scan(*coeffs(xcc, 1), reverse=True)
    end_b = hc[0:1]

    def fwd(t, carry):
        r0 = pl.multiple_of(t * R, R)
        a, u = coeffs(conv(xpad_sc[pl.ds(r0, R + 2 * pad)]), 0)
        a, u = scan(a, u, reverse=False)
        h = u + a * carry
        hf_sc[pl.ds(r0, R)] = h
        return h[R - 1:R]
    lax.fori_loop(0, n_tiles, fwd, end_f)

    def bwd(tt, carry):
        r0 = pl.multiple_of((n_tiles - 1 - tt) * R, R)
        a, u = coeffs(conv(xpad_sc[pl.ds(r0, R + 2 * pad)]), 1)
        a, u = scan(a, u, reverse=True)
        h = u + a * carry
        y = gg_ref[pl.ds(r0, R)].astype(f32) * (hf_sc[pl.ds(r0, R)] + h)
        y_ref[pl.ds(r0, R)] = y.astype(bf16)
        return h[0:1]
    lax.fori_loop(0, n_tiles, bwd, end_b)


def _rglru(xr, gg, conv_w, conv_b, wa, wx, ba, bx, lam):
    blk = RNN_BLOCK
    ctx_blk0 = N_LAT // CTX_LEN
    vec = lambda rows: pl.BlockSpec((rows, blk), lambda b, c: (0, c))
    wspec = pl.BlockSpec((2, 1, blk, blk), lambda b, c: (0, c, 0, 0))
    return pl.pallas_call(
        _rglru_kernel,
        grid=(BATCH, N_RNN_BLOCKS),
        in_specs=[pl.BlockSpec((SEQ, blk), lambda b, c: (b, c)),
                  pl.BlockSpec((CTX_LEN, blk), lambda b, c: (ctx_blk0 + b, c)),
                  pl.BlockSpec((SEQ, blk), lambda b, c: (b, c)),
                  vec(CONV_W), vec(1), wspec, wspec, vec(2), vec(2), vec(2)],
        out_specs=pl.BlockSpec((SEQ, blk), lambda b, c: (b, c)),
        out_shape=jax.ShapeDtypeStruct((N_LAT, D_RNN), bf16),
        scratch_shapes=[pltpu.VMEM((SEQ + 16, blk), f32),
                        pltpu.VMEM((CTX_LEN + 16, blk), f32),
                        pltpu.VMEM((SEQ, blk), f32)],
        compiler_params=pltpu.CompilerParams(dimension_semantics=("parallel", "parallel"),
                                             vmem_limit_bytes=VMEM_LIMIT),
        name="rglru_scan",
    )(xr, xr, gg, conv_w, conv_b, wa, wx, ba, bx, lam)


def _rope_tables():
    t = jnp.arange(SEQ, dtype=jnp.int32)
    n_freq = HEAD_DIM // 4
    inv_freq = ROPE_THETA ** (-jnp.arange(n_freq, dtype=f32) / n_freq)
    ang_r = (t // GRID_W).astype(f32)[:, None] * inv_freq
    ang_c = (t % GRID_W).astype(f32)[:, None] * inv_freq
    cos = jnp.concatenate([jnp.cos(ang_r)] * 2 + [jnp.cos(ang_c)] * 2, axis=1)
    sin = jnp.concatenate([-jnp.sin(ang_r), jnp.sin(ang_r), -jnp.sin(ang_c), jnp.sin(ang_c)], axis=1)
    cos = jnp.concatenate([cos, jnp.ones((TB, HEAD_DIM), f32)], axis=0)
    sin = jnp.concatenate([sin, jnp.zeros((TB, HEAD_DIM), f32)], axis=0)
    return cos, sin


def _peer(h2, x1, mods, wq, keys, u_tab, v_tab, final_g, final_norm):
    tabs = _peer_route(h2, wq.T.astype(bf16),
                       keys.reshape(2 * PEER_HEADS, N_KEYS, D_KEY // 2).astype(bf16))
    return _peer_dense(h2, tabs, u_tab.astype(bf16), v_tab.T.astype(bf16), x1, mods,
                       final_g.reshape(1, D_MODEL), final_norm)


def kernel(x, c, ctx, c_ctx, mod_w, mod_b, norm1_g, norm2_g, ab_w_in, gmlp_norm_g, gmlp_ws, gmlp_bs,
           q_norm_g, k_norm_g, ab_w_out, rg_w_in, rg_conv_w, rg_conv_b, rg_wa, rg_ba, rg_wx, rg_bx,
           rg_lambda, rg_w_out, peer_wq, peer_keys, peer_u, peer_v, final_norm_g):
    xs = jnp.concatenate([x.reshape(N_LAT, D_MODEL), ctx.reshape(BATCH * CTX_LEN, D_MODEL)], axis=0)
    cvec = jnp.concatenate([c, c_ctx[None, :], jnp.zeros((8 - BATCH - 1, D_MODEL), f32)], axis=0)
    mods = _mod_vectors(cvec, mod_w, mod_b)
    cosf, sinf = _rope_tables()
    row = lambda v: v.reshape(1, -1)

    gbias = jnp.broadcast_to(gmlp_bs[0][:, :, None], (A_GROUPS, CHUNK, CHUNK))
    ya, q, k, v = _ab_in(xs, mods[0], row(norm1_g[0]), ab_w_in[0].astype(bf16), row(gmlp_norm_g[0]),
                         gmlp_ws[0].astype(bf16), gbias, row(q_norm_g[0]), row(k_norm_g[0]), cosf, sinf)
    o = _attention(q, k, v)
    w_out = ab_w_out[0].astype(bf16)
    x1, h2 = _out_proj([ya, o], [w_out[:A_WIDTH], w_out[A_WIDTH:]], xs, mods[0], row(norm2_g[0]), NB_ALL)
    xs = _peer(h2, x1, mods[0], peer_wq[0], peer_keys[0], peer_u[0], peer_v[0], final_norm_g, False)

    gg, xr = _rg_in(xs, mods[1], row(norm1_g[1]), rg_w_in[0].astype(bf16))
    y = _rglru(xr, gg, rg_conv_w[0], row(rg_conv_b[0]), rg_wa[0].astype(bf16), rg_wx[0].astype(bf16),
               rg_ba[0], rg_bx[0], rg_lambda[0])
    x1, h2 = _out_proj([y], [rg_w_out[0].astype(bf16)], xs, mods[1], row(norm2_g[1]), NB_LAT)
    out = _peer(h2, x1, mods[1], peer_wq[1], peer_keys[1], peer_u[1], peer_v[1], final_norm_g, True)
    return out.reshape(BATCH, SEQ, D_MODEL)
```

```python
import functools

import jax
import jax.numpy as jnp
from jax import lax
from jax.experimental import pallas as pl
from jax.experimental.pallas import tpu as pltpu

f32 = jnp.float32
bf16 = jnp.bfloat16

D_MODEL = 1024
BATCH = 2
SEQ = 8192
DEPTH = 2
GRID_W = 64
CTX_LEN = 256
EPS = 1e-6
N_MOD = 6
A_GROUPS = 4
A_WIDTH = 512
CHUNK = 128
N_Q_HEADS = 4
N_KV_HEADS = 2
HEAD_DIM = 128
ROPE_THETA = 10000.0
B_WIDTH = N_Q_HEADS * HEAD_DIM
KV_WIDTH = N_KV_HEADS * HEAD_DIM
AB_IN = 2 * A_WIDTH + B_WIDTH + 2 * KV_WIDTH
D_RNN = 1280
RNN_BLOCK = 128
N_RNN_BLOCKS = D_RNN // RNN_BLOCK
CONV_W = 4
RG_C = 8.0
PEER_HEADS = 8
N_KEYS = 128
N_EXPERTS = N_KEYS * N_KEYS
PEER_TOPK = 16
D_KEY = 256

N_LAT = BATCH * SEQ
N_TOK = N_LAT + BATCH * CTX_LEN
TB = 256
NB_LAT = N_LAT // TB
NB_ALL = N_TOK // TB
NB_SEQ = SEQ // TB
ATT_CK = 1024
PEER_TT = 512
PEER_TC = 256
PEER_EB = 1024
SCAN_R = 256
VMEM_LIMIT = 56 * 1024 * 1024

MOD_SH1, MOD_SC1, MOD_G1, MOD_SH2, MOD_SC2, MOD_G2 = range(6)

_NT = (((1,), (1,)), ((), ()))


def _rms(x, g):
    return x * lax.rsqrt(jnp.mean(x * x, axis=-1, keepdims=True) + EPS) * g


def _gelu(x):
    hx = 0.5 * x
    return hx + hx * jnp.tanh(x * (0.7978845608028654 + 0.035677408136300125 * (x * x)))


def _mod_row_of_block(j):
    return jnp.where(j < NB_LAT, j // NB_SEQ, 2)


def _mrow(mod_ref, k):
    return mod_ref[0, k:k + 1, :]


def _mod_kernel(c_ref, w_ref, b_ref, o_ref):
    c = c_ref[...]
    s = c * jax.nn.sigmoid(c)
    o_ref[0] = jnp.dot(s, w_ref[0], precision=lax.Precision.HIGHEST,
                       preferred_element_type=f32) + b_ref[0]


def _mod_vectors(cvec, mod_w, mod_b):
    nb = 1536
    width = N_MOD * D_MODEL
    out = pl.pallas_call(
        _mod_kernel,
        grid=(DEPTH, width // nb),
        in_specs=[pl.BlockSpec((8, D_MODEL), lambda l, n: (0, 0)),
                  pl.BlockSpec((1, D_MODEL, nb), lambda l, n: (l, 0, n)),
                  pl.BlockSpec((1, 1, nb), lambda l, n: (l, 0, n))],
        out_specs=pl.BlockSpec((1, 8, nb), lambda l, n: (l, 0, n)),
        out_shape=jax.ShapeDtypeStruct((DEPTH, 8, width), f32),
        name="mod_vectors",
    )(cvec, mod_w, mod_b.reshape(DEPTH, 1, width))
    mods = out.reshape(DEPTH, 8, N_MOD, D_MODEL)[:, :3]
    return jnp.pad(mods, ((0, 0), (0, 0), (0, 8 - N_MOD), (0, 0)))


def _ab_in_kernel(x_ref, mod_ref, n1_ref, win_ref, gmg_ref, ws_ref, gb_ref, qg_ref, kg_ref,
                  cos_ref, sin_ref, ya_ref, q_ref, k_ref, v_ref, z_sc):
    h = _rms(x_ref[...], n1_ref[...]) * (1.0 + _mrow(mod_ref, MOD_SC1)) + _mrow(mod_ref, MOD_SH1)
    z_sc[...] = jnp.dot(h.astype(bf16), win_ref[...], preferred_element_type=f32)

    cosf = cos_ref[...]
    sinf = sin_ref[...]
    lane = lax.broadcasted_iota(jnp.int32, (TB, HEAD_DIM), 1)
    low_part = (lane & 32) == 0

    def rope(t):
        partner = jnp.where(low_part, pltpu.roll(t, HEAD_DIM - 32, 1), pltpu.roll(t, 32, 1))
        return t * cosf + partner * sinf

    for g in range(A_GROUPS):
        lo = g * CHUNK
        u = _gelu(z_sc[:, lo:lo + CHUNK])
        vg = _gelu(z_sc[:, A_WIDTH + lo:A_WIDTH + lo + CHUNK])
        vn = _rms(vg, gmg_ref[:, lo:lo + CHUNK]).astype(bf16)
        for c in range(TB // CHUNK):
            r0 = c * CHUNK
            s = jnp.dot(ws_ref[g], vn[r0:r0 + CHUNK], preferred_element_type=f32) + gb_ref[g]
            ya_ref[r0:r0 + CHUNK, lo:lo + CHUNK] = (u[r0:r0 + CHUNK] * s).astype(bf16)

    q0 = 2 * A_WIDTH
    for hq in range(N_Q_HEADS):
        lo = hq * HEAD_DIM
        qh = _rms(z_sc[:, q0 + lo:q0 + lo + HEAD_DIM], qg_ref[...])
        q_ref[:, lo:lo + HEAD_DIM] = rope(qh).astype(bf16)
    k0 = q0 + B_WIDTH
    for hk in range(N_KV_HEADS):
        lo = hk * HEAD_DIM
        kh = _rms(z_sc[:, k0 + lo:k0 + lo + HEAD_DIM], kg_ref[...])
        k_ref[:, lo:lo + HEAD_DIM] = rope(kh).astype(bf16)
    v0 = k0 + KV_WIDTH
    v_ref[...] = z_sc[:, v0:v0 + KV_WIDTH].astype(bf16)


def _ab_in(xs, mods, n1, w_in, gm_g, ws, gbias, qg, kg, cosf, sinf):
    full = lambda shape: pl.BlockSpec(shape, lambda j: (0,) * len(shape))
    rope_blk = lambda j: (jnp.where(j < NB_LAT, j % NB_SEQ, NB_SEQ), 0)
    return pl.pallas_call(
        _ab_in_kernel,
        grid=(NB_ALL,),
        in_specs=[pl.BlockSpec((TB, D_MODEL), lambda j: (j, 0)),
                  pl.BlockSpec((1, 8, D_MODEL), lambda j: (_mod_row_of_block(j), 0, 0)),
                  full((1, D_MODEL)), full((D_MODEL, AB_IN)), full((1, A_WIDTH)),
                  full((A_GROUPS, CHUNK, CHUNK)), full((A_GROUPS, CHUNK, CHUNK)),
                  full((1, HEAD_DIM)), full((1, HEAD_DIM)),
                  pl.BlockSpec((TB, HEAD_DIM), rope_blk), pl.BlockSpec((TB, HEAD_DIM), rope_blk)],
        out_specs=[pl.BlockSpec((TB, A_WIDTH), lambda j: (j, 0)),
                   pl.BlockSpec((TB, B_WIDTH), lambda j: (j, 0)),
                   pl.BlockSpec((TB, KV_WIDTH), lambda j: (j, 0)),
                   pl.BlockSpec((TB, KV_WIDTH), lambda j: (j, 0))],
        out_shape=[jax.ShapeDtypeStruct((N_TOK, A_WIDTH), bf16),
                   jax.ShapeDtypeStruct((N_TOK, B_WIDTH), bf16),
                   jax.ShapeDtypeStruct((N_TOK, KV_WIDTH), bf16),
                   jax.ShapeDtypeStruct((N_TOK, KV_WIDTH), bf16)],
        scratch_shapes=[pltpu.VMEM((TB, AB_IN), f32)],
        compiler_params=pltpu.CompilerParams(dimension_semantics=("parallel",),
                                             vmem_limit_bytes=VMEM_LIMIT),
        name="ab_in_proj",
    )(xs, mods, n1, w_in, gm_g, ws, gbias, qg, kg, cosf, sinf)


def _attn_kernel(q_ref, kl_ref, kc_ref, vl_ref, vc_ref, o_ref, m_sc, l_sc, acc_sc, s_sc, p_sc):
    j = pl.program_id(0)
    scale = HEAD_DIM ** -0.5
    group = N_Q_HEADS // N_KV_HEADS
    n_chunks = SEQ // ATT_CK

    def scores(hq, k_blk):
        return lax.dot_general(q_ref[:, hq * HEAD_DIM:(hq + 1) * HEAD_DIM], k_blk, _NT,
                               preferred_element_type=f32) * scale

    for hq in range(N_Q_HEADS):
        klo = (hq // group) * HEAD_DIM
        s = scores(hq, kc_ref[:, klo:klo + HEAD_DIM])
        m = jnp.max(s, axis=-1, keepdims=True)
        p = jnp.exp(s - m)
        m_sc[hq] = m
        l_sc[hq] = jnp.sum(p, axis=-1, keepdims=True)
        acc_sc[hq] = jnp.dot(p.astype(bf16), vc_ref[:, klo:klo + HEAD_DIM], preferred_element_type=f32)

    @pl.when(j < NB_LAT)
    def _():
        for hq in range(N_Q_HEADS):
            klo = (hq // group) * HEAD_DIM
            s_sc[hq] = scores(hq, kl_ref[0:ATT_CK, klo:klo + HEAD_DIM])

        def step(i, carry):
            r0 = pl.multiple_of(i * ATT_CK, ATT_CK)
            rn = pl.multiple_of(jnp.minimum(i + 1, n_chunks - 1) * ATT_CK, ATT_CK)
            for hq in range(N_Q_HEADS):
                klo = (hq // group) * HEAD_DIM
                s = s_sc[hq]
                m_old = m_sc[hq]
                m_new = jnp.maximum(m_old, jnp.max(s, axis=-1, keepdims=True))
                alpha = jnp.exp(m_old - m_new)
                p = jnp.exp(s - m_new)
                l_sc[hq] = alpha * l_sc[hq] + jnp.sum(p, axis=-1, keepdims=True)
                m_sc[hq] = m_new
                p_sc[hq] = p.astype(bf16)
                acc_sc[hq] = alpha * acc_sc[hq] + jnp.dot(
                    p_sc[hq], vl_ref[pl.ds(r0, ATT_CK), klo:klo + HEAD_DIM], preferred_element_type=f32)
                s_sc[hq] = scores(hq, kl_ref[pl.ds(rn, ATT_CK), klo:klo + HEAD_DIM])
            return carry
        lax.fori_loop(0, n_chunks, step, 0)

    for hq in range(N_Q_HEADS):
        o_ref[:, hq * HEAD_DIM:(hq + 1) * HEAD_DIM] = (acc_sc[hq] / l_sc[hq]).astype(bf16)


def _attention(q, k, v):
    batch_of = lambda j: jnp.where(j < NB_LAT, j // NB_SEQ, j - NB_LAT)
    lat_spec = pl.BlockSpec((SEQ, KV_WIDTH), lambda j: (batch_of(j), 0))
    ctx_spec = pl.BlockSpec((CTX_LEN, KV_WIDTH), lambda j: (N_LAT // CTX_LEN + batch_of(j), 0))
    return pl.pallas_call(
        _attn_kernel,
        grid=(NB_ALL,),
        in_specs=[pl.BlockSpec((TB, B_WIDTH), lambda j: (j, 0)), lat_spec, ctx_spec, lat_spec, ctx_spec],
        out_specs=pl.BlockSpec((TB, B_WIDTH), lambda j: (j, 0)),
        out_shape=jax.ShapeDtypeStruct((N_TOK, B_WIDTH), bf16),
        scratch_shapes=[pltpu.VMEM((N_Q_HEADS, TB, 1), f32), pltpu.VMEM((N_Q_HEADS, TB, 1), f32),
                        pltpu.VMEM((N_Q_HEADS, TB, HEAD_DIM), f32),
                        pltpu.VMEM((N_Q_HEADS, TB, ATT_CK), f32), pltpu.VMEM((N_Q_HEADS, TB, ATT_CK), bf16)],
        compiler_params=pltpu.CompilerParams(dimension_semantics=("parallel",),
                                             vmem_limit_bytes=VMEM_LIMIT),
        name="gqa_attention",
    )(q, k, k, v, v)


def _out_proj_kernel(*refs, n_in):
    a_refs = refs[:n_in]
    w_refs = refs[n_in:2 * n_in]
    x_ref, mod_ref, n2_ref, x1_ref, h2_ref = refs[2 * n_in:]
    y = jnp.dot(a_refs[0][...], w_refs[0][...], preferred_element_type=f32)
    for a_ref, w_ref in zip(a_refs[1:], w_refs[1:]):
        y = y + jnp.dot(a_ref[...], w_ref[...], preferred_element_type=f32)
    x1 = x_ref[...] + _mrow(mod_ref, MOD_G1) * y
    x1_ref[...] = x1
    h2 = _rms(x1, n2_ref[...]) * (1.0 + _mrow(mod_ref, MOD_SC2)) + _mrow(mod_ref, MOD_SH2)
    h2_ref[...] = h2.astype(bf16)


def _out_proj(acts, weights, xs, mods, n2, n_blocks):
    n_rows = n_blocks * TB
    in_specs = [pl.BlockSpec((TB, a.shape[1]), lambda j: (j, 0)) for a in acts]
    in_specs += [pl.BlockSpec(w.shape, lambda j: (0, 0)) for w in weights]
    in_specs += [pl.BlockSpec((TB, D_MODEL), lambda j: (j, 0)),
                 pl.BlockSpec((1, 8, D_MODEL), lambda j: (_mod_row_of_block(j), 0, 0)),
                 pl.BlockSpec((1, D_MODEL), lambda j: (0, 0))]
    return pl.pallas_call(
        functools.partial(_out_proj_kernel, n_in=len(acts)),
        grid=(n_blocks,),
        in_specs=in_specs,
        out_specs=[pl.BlockSpec((TB, D_MODEL), lambda j: (j, 0)),
                   pl.BlockSpec((TB, D_MODEL), lambda j: (j, 0))],
        out_shape=[jax.ShapeDtypeStruct((n_rows, D_MODEL), f32),
                   jax.ShapeDtypeStruct((n_rows, D_MODEL), bf16)],
        compiler_params=pltpu.CompilerParams(dimension_semantics=("parallel",),
                                             vmem_limit_bytes=VMEM_LIMIT),
        name="mixer_out_proj",
    )(*acts, *weights, xs, mods, n2)


def _oddeven_merge(lo, hi, r):
    step = r * 2
    if step < hi - lo:
        yield from _oddeven_merge(lo, hi, step)
        yield from _oddeven_merge(lo + r, hi, step)
        yield from ((i, i + r) for i in range(lo + r, hi - r, step))
    else:
        yield (lo, lo + r)


def _oddeven_merge_sort(lo, hi):
    if hi - lo >= 1:
        mid = lo + (hi - lo) // 2
        yield from _oddeven_merge_sort(lo, mid)
        yield from _oddeven_merge_sort(mid + 1, hi)
        yield from _oddeven_merge(lo, hi, 1)


_SORT_TOPK = tuple(_oddeven_merge_sort(0, PEER_TOPK - 1))


def _compare_exchange(v, i, j):
    v[i], v[j] = jnp.maximum(v[i], v[j]), jnp.minimum(v[i], v[j])


def _merge_top(x, y):
    n = PEER_TOPK
    z = list(x)
    for i in range(n - len(y), n):
        z[i] = jnp.maximum(x[i], y[n - 1 - i])
    d = n // 2
    while d >= 1:
        for i in range(n):
            if i & d == 0:
                _compare_exchange(z, i, i + d)
        d //= 2
    return z


def _pair_allowed(k, l):
    return (k + 1) * (l + 1) <= PEER_TOPK


def _route_kernel(h_ref, wqt_ref, keys_ref, c0_ref, e0_ref, r1_ref, e1_ref, q_sc, s_sc, a_sc, b_sc, z_sc):
    n = PEER_TOPK
    sub = 8
    q_sc[...] = lax.dot_general(wqt_ref[...], h_ref[...], _NT,
                                preferred_element_type=f32).astype(bf16)

    def half_topk(hh, carry):
        r0 = pl.multiple_of(hh * N_KEYS, N_KEYS)
        s = jnp.dot(keys_ref[hh], q_sc[pl.ds(r0, N_KEYS), :], preferred_element_type=f32)
        s_sc[hh] = s
        v = [s[sub * i:sub * (i + 1)] for i in range(N_KEYS // sub)]
        for i, j in _SORT_TOPK:
            _compare_exchange(v, i, j)
        shift = sub // 2
        while shift >= 1:
            v = _merge_top(v, [pltpu.roll(t, shift, 0) for t in v])
            shift //= 2
        slot = (hh % 2) * PEER_HEADS + hh // 2
        for k in range(n):
            a_sc[k, pl.ds(slot, 1), :] = v[k][0:1, :]
        return carry

    lax.fori_loop(0, 2 * PEER_HEADS, half_topk, 0)

    a0 = [a_sc[k, 0:PEER_HEADS, :] for k in range(n)]
    a1 = [a_sc[k, PEER_HEADS:2 * PEER_HEADS, :] for k in range(n)]
    lists = [[a0[0] + a1[l] for l in range(n)],
             [a0[1] + a1[l] for l in range(n) if _pair_allowed(1, l)],
             [a0[k] + a1[0] for k in range(2, n)],
             [a0[k] + a1[1] for k in range(2, n) if _pair_allowed(k, 1)]]
    lists += [[a0[k] + a1[l] for l in range(2, n) if _pair_allowed(k, l)] for k in range(2, n)]
    top = lists[0]
    for cand in lists[1:]:
        if cand:
            top = _merge_top(top, cand)
    tau = top[n - 1]
    z = jnp.ones_like(tau)
    for k in range(1, n):
        z = z + jnp.exp(top[k] - top[0])
    z_sc[...] = z
    for l in range(n):
        bound = jnp.full((PEER_HEADS, TB), jnp.inf, f32)
        for k in range(n):
            if _pair_allowed(k, l):
                bound = jnp.where(a0[k] + a1[l] >= tau, a0[k], bound)
        b_sc[l] = bound

    def head_tables(h, carry):
        s0 = s_sc[2 * h]
        s1 = s_sc[2 * h + 1]
        count0 = jnp.zeros((N_KEYS, TB), f32)
        rank1 = jnp.zeros((N_KEYS, TB), f32)
        for k in range(n):
            count0 = count0 + jnp.where(s0 >= b_sc[k, pl.ds(h, 1), :], 1.0, 0.0)
            rank1 = rank1 + jnp.where(a_sc[k, pl.ds(PEER_HEADS + h, 1), :] > s1, 1.0, 0.0)
        inv_z = 1.0 / z_sc[pl.ds(h, 1), :]
        e0 = jnp.where(s0 >= a_sc[n - 1, pl.ds(h, 1), :], jnp.exp(s0 - a_sc[0, pl.ds(h, 1), :]), 0.0)
        e1 = jnp.where(s1 >= a_sc[n - 1, pl.ds(PEER_HEADS + h, 1), :],
                       jnp.exp(s1 - a_sc[0, pl.ds(PEER_HEADS + h, 1), :]), 0.0)
        c0_ref[h] = count0
        e0_ref[h] = e0 * inv_z
        r1_ref[h] = rank1.astype(bf16)
        e1_ref[h] = e1.astype(bf16)
        return carry

    lax.fori_loop(0, PEER_HEADS, head_tables, 0)


def _peer_route(h2, wq_t, keys):
    n_rows = h2.shape[0]
    tab = lambda dtype: jax.ShapeDtypeStruct((PEER_HEADS, N_KEYS, n_rows), dtype)
    tab_spec = pl.BlockSpec((PEER_HEADS, N_KEYS, TB), lambda j: (0, 0, j))
    return pl.pallas_call(
        _route_kernel,
        grid=(n_rows // TB,),
        in_specs=[pl.BlockSpec((TB, D_MODEL), lambda j: (j, 0)),
                  pl.BlockSpec(wq_t.shape, lambda j: (0, 0)),
                  pl.BlockSpec(keys.shape, lambda j: (0, 0, 0))],
        out_specs=[tab_spec] * 4,
        out_shape=[tab(f32), tab(f32), tab(bf16), tab(bf16)],
        scratch_shapes=[pltpu.VMEM((2 * PEER_HEADS * N_KEYS, TB), bf16),
                        pltpu.VMEM((2 * PEER_HEADS, N_KEYS, TB), f32),
                        pltpu.VMEM((PEER_TOPK, 2 * PEER_HEADS, TB), f32),
                        pltpu.VMEM((PEER_TOPK, PEER_HEADS, TB), f32),
                        pltpu.VMEM((PEER_HEADS, TB), f32)],
        compiler_params=pltpu.CompilerParams(dimension_semantics=("parallel",),
                                             vmem_limit_bytes=VMEM_LIMIT),
        name="peer_route",
    )(h2, wq_t, keys)


def _peer_dense_kernel(h_ref, u_ref, vt_ref, c0_ref, e0_ref, r1_ref, e1_ref, x1_ref, mod_ref, fg_ref,
                       o_ref, acc_sc, sc_sc, act_sc, *, final_norm):
    g = pl.program_id(1)

    @pl.when(g == 0)
    def _():
        acc_sc[...] = jnp.zeros_like(acc_sc)
        sc_sc[...] = jnp.zeros_like(sc_sc)

    pk = 16
    zero = jnp.zeros((pk, PEER_TC), bf16)
    for tc in range(PEER_TT // PEER_TC):
        tok = slice(tc * PEER_TC, (tc + 1) * PEER_TC)
        for ii in range(PEER_EB // N_KEYS):
            c0 = [jnp.broadcast_to(c0_ref[h, ii:ii + 1, tok], (pk, PEER_TC)).astype(bf16)
                  for h in range(PEER_HEADS)]
            e0 = [jnp.broadcast_to(e0_ref[h, ii:ii + 1, tok], (pk, PEER_TC)).astype(bf16)
                  for h in range(PEER_HEADS)]
            for q in range(N_KEYS // pk):
                lo = q * pk
                gate = zero
                for h in range(PEER_HEADS):
                    picked = r1_ref[h, lo:lo + pk, tok] < c0[h]
                    gate = gate + jnp.where(picked, e1_ref[h, lo:lo + pk, tok], zero) * e0[h]
                row = ii * N_KEYS + lo
                act_sc[row:row + pk, tok] = _gelu(sc_sc[row:row + pk, tok]).astype(bf16) * gate
        acc_sc[:, tok] += jnp.dot(vt_ref[...], act_sc[:, tok], preferred_element_type=f32)
        sc_sc[:, tok] = lax.dot_general(u_ref[...], h_ref[tok, :], _NT, preferred_element_type=f32)

    @pl.when(g == pl.num_programs(1) - 1)
    def _():
        x2 = x1_ref[...] + _mrow(mod_ref, MOD_G2) * acc_sc[...].T
        if final_norm:
            x2 = _rms(x2, fg_ref[...])
        o_ref[...] = x2


def _peer_dense(h2, tabs, u_tab, v_tab_t, x1, mods, final_g, final_norm):
    n_rows = h2.shape[0]
    c0, e0, r1, e1 = tabs
    tiles_per_seq = SEQ // PEER_TT
    n_eb = N_EXPERTS // PEER_EB
    mod_row = lambda t, g: (jnp.where(t < N_LAT // PEER_TT, t // tiles_per_seq, 2), 0, 0)
    cur = lambda g: jnp.minimum(g, n_eb - 1)
    prev = lambda g: jnp.maximum(g - 1, 0)
    row_spec = pl.BlockSpec((PEER_HEADS, PEER_EB // N_KEYS, PEER_TT), lambda t, g: (0, prev(g), t))
    key_spec = pl.BlockSpec((PEER_HEADS, N_KEYS, PEER_TT), lambda t, g: (0, 0, t))
    return pl.pallas_call(
        functools.partial(_peer_dense_kernel, final_norm=final_norm),
        grid=(n_rows // PEER_TT, n_eb + 1),
        in_specs=[pl.BlockSpec((PEER_TT, D_MODEL), lambda t, g: (t, 0)),
                  pl.BlockSpec((PEER_EB, D_MODEL), lambda t, g: (cur(g), 0)),
                  pl.BlockSpec((D_MODEL, PEER_EB), lambda t, g: (0, prev(g))),
                  row_spec, row_spec, key_spec, key_spec,
                  pl.BlockSpec((PEER_TT, D_MODEL), lambda t, g: (t, 0)),
                  pl.BlockSpec((1, 8, D_MODEL), mod_row),
                  pl.BlockSpec((1, D_MODEL), lambda t, g: (0, 0))],
        out_specs=pl.BlockSpec((PEER_TT, D_MODEL), lambda t, g: (t, 0)),
        out_shape=jax.ShapeDtypeStruct((n_rows, D_MODEL), f32),
        scratch_shapes=[pltpu.VMEM((D_MODEL, PEER_TT), f32),
                        pltpu.VMEM((PEER_EB, PEER_TT), f32),
                        pltpu.VMEM((PEER_EB, PEER_TT), bf16)],
        compiler_params=pltpu.CompilerParams(dimension_semantics=("parallel", "arbitrary"),
                                             vmem_limit_bytes=VMEM_LIMIT),
        name="peer_dense",
    )(h2, u_tab, v_tab_t, c0, e0, r1, e1, x1, mods, final_g)


def _rg_in_kernel(x_ref, mod_ref, n1_ref, win_ref, gg_ref, xr_ref):
    h = _rms(x_ref[...], n1_ref[...]) * (1.0 + _mrow(mod_ref, MOD_SC1)) + _mrow(mod_ref, MOD_SH1)
    z = jnp.dot(h.astype(bf16), win_ref[...], preferred_element_type=f32)
    gg_ref[...] = _gelu(z[:, :D_RNN]).astype(bf16)
    xr_ref[...] = z[:, D_RNN:]


def _rg_in(xs, mods, n1, w_in):
    return pl.pallas_call(
        _rg_in_kernel,
        grid=(NB_ALL,),
        in_specs=[pl.BlockSpec((TB, D_MODEL), lambda j: (j, 0)),
                  pl.BlockSpec((1, 8, D_MODEL), lambda j: (_mod_row_of_block(j), 0, 0)),
                  pl.BlockSpec((1, D_MODEL), lambda j: (0, 0)),
                  pl.BlockSpec((D_MODEL, 2 * D_RNN), lambda j: (0, 0))],
        out_specs=[pl.BlockSpec((TB, D_RNN), lambda j: (j, 0)),
                   pl.BlockSpec((TB, D_RNN), lambda j: (j, 0))],
        out_shape=[jax.ShapeDtypeStruct((N_TOK, D_RNN), bf16),
                   jax.ShapeDtypeStruct((N_TOK, D_RNN), f32)],
        compiler_params=pltpu.CompilerParams(dimension_semantics=("parallel",),
                                             vmem_limit_bytes=VMEM_LIMIT),
        name="rglru_in_proj",
    )(xs, mods, n1, w_in)


def _rglru_kernel(xl_ref, xc_ref, gg_ref, cw_ref, cb_ref, wa_ref, wx_ref, ba_ref, bx_ref, lam_ref,
                  y_ref, xpad_sc, cpad_sc, hf_sc):
    R = SCAN_R
    n_tiles = SEQ // R
    pad = 8
    zeros = jnp.zeros((pad, RNN_BLOCK), f32)
    xpad_sc[0:pad] = zeros
    xpad_sc[pad + SEQ:2 * pad + SEQ] = zeros
    cpad_sc[0:pad] = zeros
    cpad_sc[pad + CTX_LEN:2 * pad + CTX_LEN] = zeros
    cpad_sc[pad:pad + CTX_LEN] = xc_ref[...]

    def copy_in(t, carry):
        r0 = pl.multiple_of(t * R, R)
        xpad_sc[pl.ds(r0 + pad, R)] = xl_ref[pl.ds(r0, R)]
        return carry
    lax.fori_loop(0, n_tiles, copy_in, 0)

    cw = cw_ref[...]
    cb = cb_ref[...]

    def conv(win):
        acc = cb
        for w in range(CONV_W):
            lo = pad - CONV_W // 2 + w
            acc = acc + win[lo:lo + R] * cw[w:w + 1]
        return acc

    def coeffs(xc, d):
        xb = xc.astype(bf16)
        r = jax.nn.sigmoid(jnp.dot(xb, wa_ref[d, 0], preferred_element_type=f32) + ba_ref[d:d + 1, :])
        i = jax.nn.sigmoid(jnp.dot(xb, wx_ref[d, 0], preferred_element_type=f32) + bx_ref[d:d + 1, :])
        nl = -lam_ref[d:d + 1, :]
        softplus = jnp.maximum(nl, 0.0) + jnp.log1p(jnp.exp(-jnp.abs(nl)))
        log_a = (-RG_C) * r * softplus
        a = jnp.exp(log_a)
        one_minus_a2 = -jnp.tanh(log_a) * (a * a + 1.0)
        return a, jnp.sqrt(one_minus_a2) * (i * xc)

    row = lax.broadcasted_iota(jnp.int32, (R, RNN_BLOCK), 0)

    def scan(a, u, reverse):
        s = 1
        while s < R:
            if reverse:
                edge = row >= R - s
                a_prev = pltpu.roll(a, R - s, 0)
                u_prev = pltpu.roll(u, R - s, 0)
            else:
                edge = row < s
                a_prev = pltpu.roll(a, s, 0)
                u_prev = pltpu.roll(u, s, 0)
            u = a * jnp.where(edge, 0.0, u_prev) + u
            a = a * jnp.where(edge, 1.0, a_prev)
            s *= 2
        return a, u

    xcc = conv(cpad_sc[...])
    _, hc = scan(*coeffs(xcc, 0), reverse=False)
    end_f = hc[R - 1:R]
    _, hc = scan(*coeffs(xcc, 1), reverse=True)
    end_b = hc[0:1]

    def fwd(t, carry):
        r0 = pl.multiple_of(t * R, R)
        a, u = coeffs(conv(xpad_sc[pl.ds(r0, R + 2 * pad)]), 0)
        a, u = scan(a, u, reverse=False)
        h = u + a * carry
        hf_sc[pl.ds(r0, R)] = h
        return h[R - 1:R]
    lax.fori_loop(0, n_tiles, fwd, end_f)

    def bwd(tt, carry):
        r0 = pl.multiple_of((n_tiles - 1 - tt) * R, R)
        a, u = coeffs(conv(xpad_sc[pl.ds(r0, R + 2 * pad)]), 1)
        a, u = scan(a, u, reverse=True)
        h = u + a * carry
        y = gg_ref[pl.ds(r0, R)].astype(f32) * (hf_sc[pl.ds(r0, R)] + h)
        y_ref[pl.ds(r0, R)] = y.astype(bf16)
        return h[0:1]
    lax.fori_loop(0, n_tiles, bwd, end_b)


def _rglru(xr, gg, conv_w, conv_b, wa, wx, ba, bx, lam):
    blk = RNN_BLOCK
    ctx_blk0 = N_LAT // CTX_LEN
    vec = lambda rows: pl.BlockSpec((rows, blk), lambda b, c: (0, c))
    wspec = pl.BlockSpec((2, 1, blk, blk), lambda b, c: (0, c, 0, 0))
    return pl.pallas_call(
        _rglru_kernel,
        grid=(BATCH, N_RNN_BLOCKS),
        in_specs=[pl.BlockSpec((SEQ, blk), lambda b, c: (b, c)),
                  pl.BlockSpec((CTX_LEN, blk), lambda b, c: (ctx_blk0 + b, c)),
                  pl.BlockSpec((SEQ, blk), lambda b, c: (b, c)),
                  vec(CONV_W), vec(1), wspec, wspec, vec(2), vec(2), vec(2)],
        out_specs=pl.BlockSpec((SEQ, blk), lambda b, c: (b, c)),
        out_shape=jax.ShapeDtypeStruct((N_LAT, D_RNN), bf16),
        scratch_shapes=[pltpu.VMEM((SEQ + 16, blk), f32),
                        pltpu.VMEM((CTX_LEN + 16, blk), f32),
                        pltpu.VMEM((SEQ, blk), f32)],
        compiler_params=pltpu.CompilerParams(dimension_semantics=("parallel", "parallel"),
                                             vmem_limit_bytes=VMEM_LIMIT),
        name="rglru_scan",
    )(xr, xr, gg, conv_w, conv_b, wa, wx, ba, bx, lam)


def _rope_tables():
    t = jnp.arange(SEQ, dtype=jnp.int32)
    n_freq = HEAD_DIM // 4
    inv_freq = ROPE_THETA ** (-jnp.arange(n_freq, dtype=f32) / n_freq)
    ang_r = (t // GRID_W).astype(f32)[:, None] * inv_freq
    ang_c = (t % GRID_W).astype(f32)[:, None] * inv_freq
    cos = jnp.concatenate([jnp.cos(ang_r)] * 2 + [jnp.cos(ang_c)] * 2, axis=1)
    sin = jnp.concatenate([-jnp.sin(ang_r), jnp.sin(ang_r), -jnp.sin(ang_c), jnp.sin(ang_c)], axis=1)
    cos = jnp.concatenate([cos, jnp.ones((TB, HEAD_DIM), f32)], axis=0)
    sin = jnp.concatenate([sin, jnp.zeros((TB, HEAD_DIM), f32)], axis=0)
    return cos, sin


def _peer(h2, x1, mods, wq, keys, u_tab, v_tab, final_g, final_norm):
    tabs = _peer_route(h2, wq.T.astype(bf16),
                       keys.reshape(2 * PEER_HEADS, N_KEYS, D_KEY // 2).astype(bf16))
    return _peer_dense(h2, tabs, u_tab.astype(bf16), v_tab.T.astype(bf16), x1, mods,
                       final_g.reshape(1, D_MODEL), final_norm)


def kernel(x, c, ctx, c_ctx, mod_w, mod_b, norm1_g, norm2_g, ab_w_in, gmlp_norm_g, gmlp_ws, gmlp_bs,
           q_norm_g, k_norm_g, ab_w_out, rg_w_in, rg_conv_w, rg_conv_b, rg_wa, rg_ba, rg_wx, rg_bx,
           rg_lambda, rg_w_out, peer_wq, peer_keys, peer_u, peer_v, final_norm_g):
    xs = jnp.concatenate([x.reshape(N_LAT, D_MODEL), ctx.reshape(BATCH * CTX_LEN, D_MODEL)], axis=0)
    cvec = jnp.concatenate([c, c_ctx[None, :], jnp.zeros((8 - BATCH - 1, D_MODEL), f32)], axis=0)
    mods = _mod_vectors(cvec, mod_w, mod_b)
    cosf, sinf = _rope_tables()
    row = lambda v: v.reshape(1, -1)

    gbias = jnp.broadcast_to(gmlp_bs[0][:, :, None], (A_GROUPS, CHUNK, CHUNK))
    ya, q, k, v = _ab_in(xs, mods[0], row(norm1_g[0]), ab_w_in[0].astype(bf16), row(gmlp_norm_g[0]),
                         gmlp_ws[0].astype(bf16), gbias, row(q_norm_g[0]), row(k_norm_g[0]), cosf, sinf)
    o = _attention(q, k, v)
    w_out = ab_w_out[0].astype(bf16)
    x1, h2 = _out_proj([ya, o], [w_out[:A_WIDTH], w_out[A_WIDTH:]], xs, mods[0], row(norm2_g[0]), NB_ALL)
    xs = _peer(h2, x1, mods[0], peer_wq[0], peer_keys[0], peer_u[0], peer_v[0], final_norm_g, False)

    gg, xr = _rg_in(xs, mods[1], row(norm1_g[1]), rg_w_in[0].astype(bf16))
    y = _rglru(xr, gg, rg_conv_w[0], row(rg_conv_b[0]), rg_wa[0].astype(bf16), rg_wx[0].astype(bf16),
               rg_ba[0], rg_bx[0], rg_lambda[0])
    x1, h2 = _out_proj([y], [rg_w_out[0].astype(bf16)], xs, mods[1], row(norm2_g[1]), NB_LAT)
    out = _peer(h2, x1, mods[1], peer_wq[1], peer_keys[1], peer_u[1], peer_v[1], final_norm_g, True)
    return out.reshape(BATCH, SEQ, D_MODEL)
```

```python
import functools

import jax
import jax.numpy as jnp
from jax import lax
from jax.experimental import pallas as pl
from jax.experimental.pallas import tpu as pltpu

f32 = jnp.float32
bf16 = jnp.bfloat16

D_MODEL = 1024
BATCH = 2
SEQ = 8192
DEPTH = 2
GRID_W = 64
CTX_LEN = 256
EPS = 1e-6
N_MOD = 6
A_GROUPS = 4
A_WIDTH = 512
CHUNK = 128
N_Q_HEADS = 4
N_KV_HEADS = 2
HEAD_DIM = 128
ROPE_THETA = 10000.0
B_WIDTH = N_Q_HEADS * HEAD_DIM
KV_WIDTH = N_KV_HEADS * HEAD_DIM
AB_IN = 2 * A_WIDTH + B_WIDTH + 2 * KV_WIDTH
D_RNN = 1280
RNN_BLOCK = 128
N_RNN_BLOCKS = D_RNN // RNN_BLOCK
CONV_W = 4
RG_C = 8.0
PEER_HEADS = 8
N_KEYS = 128
N_EXPERTS = N_KEYS * N_KEYS
PEER_TOPK = 16
D_KEY = 256

N_LAT = BATCH * SEQ
N_TOK = N_LAT + BATCH * CTX_LEN
TB = 256
NB_LAT = N_LAT // TB
NB_ALL = N_TOK // TB
NB_SEQ = SEQ // TB
ATT_TQ = CTX_LEN
ATT_NB_LAT = N_LAT // ATT_TQ
ATT_CK = 1024
ROUTE_TB = 512
PEER_TT = 1024
PEER_TC = 256
PEER_EB = 1024
SCAN_R = 256
VMEM_LIMIT = 56 * 1024 * 1024

MOD_SH1, MOD_SC1, MOD_G1, MOD_SH2, MOD_SC2, MOD_G2 = range(6)

_NT = (((1,), (1,)), ((), ()))


def _rms(x, g):
    return x * lax.rsqrt(jnp.mean(x * x, axis=-1, keepdims=True) + EPS) * g


def _gelu(x):
    hx = 0.5 * x
    return hx + hx * jnp.tanh(x * (0.7978845608028654 + 0.035677408136300125 * (x * x)))


def _mod_row_of_block(j):
    return jnp.where(j < NB_LAT, j // NB_SEQ, 2)


def _mrow(mod_ref, k):
    return mod_ref[0, k:k + 1, :]


def _mod_kernel(c_ref, w_ref, b_ref, o_ref):
    c = c_ref[...]
    s = c * jax.nn.sigmoid(c)
    o_ref[0] = jnp.dot(s, w_ref[0], precision=lax.Precision.HIGHEST,
                       preferred_element_type=f32) + b_ref[0]


def _mod_vectors(cvec, mod_w, mod_b):
    nb = 1536
    width = N_MOD * D_MODEL
    out = pl.pallas_call(
        _mod_kernel,
        grid=(DEPTH, width // nb),
        in_specs=[pl.BlockSpec((8, D_MODEL), lambda l, n: (0, 0)),
                  pl.BlockSpec((1, D_MODEL, nb), lambda l, n: (l, 0, n)),
                  pl.BlockSpec((1, 1, nb), lambda l, n: (l, 0, n))],
        out_specs=pl.BlockSpec((1, 8, nb), lambda l, n: (l, 0, n)),
        out_shape=jax.ShapeDtypeStruct((DEPTH, 8, width), f32),
        name="mod_vectors",
    )(cvec, mod_w, mod_b.reshape(DEPTH, 1, width))
    mods = out.reshape(DEPTH, 8, N_MOD, D_MODEL)[:, :3]
    return jnp.pad(mods, ((0, 0), (0, 0), (0, 8 - N_MOD), (0, 0)))


def _ab_in_kernel(x_ref, mod_ref, n1_ref, win_ref, gmg_ref, ws_ref, gb_ref, qg_ref, kg_ref,
                  cos_ref, sin_ref, ya_ref, q_ref, k_ref, v_ref, z_sc):
    h = _rms(x_ref[...], n1_ref[...]) * (1.0 + _mrow(mod_ref, MOD_SC1)) + _mrow(mod_ref, MOD_SH1)
    z_sc[...] = jnp.dot(h.astype(bf16), win_ref[...], preferred_element_type=f32)

    cosf = cos_ref[...]
    sinf = sin_ref[...]
    lane = lax.broadcasted_iota(jnp.int32, (TB, HEAD_DIM), 1)
    low_part = (lane & 32) == 0

    def rope(t):
        partner = jnp.where(low_part, pltpu.roll(t, HEAD_DIM - 32, 1), pltpu.roll(t, 32, 1))
        return t * cosf + partner * sinf

    for g in range(A_GROUPS):
        lo = g * CHUNK
        u = _gelu(z_sc[:, lo:lo + CHUNK])
        vg = _gelu(z_sc[:, A_WIDTH + lo:A_WIDTH + lo + CHUNK])
        vn = _rms(vg, gmg_ref[:, lo:lo + CHUNK]).astype(bf16)
        for c in range(TB // CHUNK):
            r0 = c * CHUNK
            s = jnp.dot(ws_ref[g], vn[r0:r0 + CHUNK], preferred_element_type=f32) + gb_ref[g]
            ya_ref[r0:r0 + CHUNK, lo:lo + CHUNK] = (u[r0:r0 + CHUNK] * s).astype(bf16)

    q0 = 2 * A_WIDTH
    for hq in range(N_Q_HEADS):
        lo = hq * HEAD_DIM
        qh = _rms(z_sc[:, q0 + lo:q0 + lo + HEAD_DIM], qg_ref[...])
        q_ref[:, lo:lo + HEAD_DIM] = rope(qh).astype(bf16)
    k0 = q0 + B_WIDTH
    for hk in range(N_KV_HEADS):
        lo = hk * HEAD_DIM
        kh = _rms(z_sc[:, k0 + lo:k0 + lo + HEAD_DIM], kg_ref[...])
        k_ref[:, lo:lo + HEAD_DIM] = rope(kh).astype(bf16)
    v0 = k0 + KV_WIDTH
    v_ref[...] = z_sc[:, v0:v0 + KV_WIDTH].astype(bf16)


def _ab_in(xs, mods, n1, w_in, gm_g, ws, gbias, qg, kg, cosf, sinf):
    full = lambda shape: pl.BlockSpec(shape, lambda j: (0,) * len(shape))
    rope_blk = lambda j: (jnp.where(j < NB_LAT, j % NB_SEQ, NB_SEQ), 0)
    return pl.pallas_call(
        _ab_in_kernel,
        grid=(NB_ALL,),
        in_specs=[pl.BlockSpec((TB, D_MODEL), lambda j: (j, 0)),
                  pl.BlockSpec((1, 8, D_MODEL), lambda j: (_mod_row_of_block(j), 0, 0)),
                  full((1, D_MODEL)), full((D_MODEL, AB_IN)), full((1, A_WIDTH)),
                  full((A_GROUPS, CHUNK, CHUNK)), full((A_GROUPS, CHUNK, CHUNK)),
                  full((1, HEAD_DIM)), full((1, HEAD_DIM)),
                  pl.BlockSpec((TB, HEAD_DIM), rope_blk), pl.BlockSpec((TB, HEAD_DIM), rope_blk)],
        out_specs=[pl.BlockSpec((TB, A_WIDTH), lambda j: (j, 0)),
                   pl.BlockSpec((TB, B_WIDTH), lambda j: (j, 0)),
                   pl.BlockSpec((TB, KV_WIDTH), lambda j: (j, 0)),
                   pl.BlockSpec((TB, KV_WIDTH), lambda j: (j, 0))],
        out_shape=[jax.ShapeDtypeStruct((N_TOK, A_WIDTH), bf16),
                   jax.ShapeDtypeStruct((N_TOK, B_WIDTH), bf16),
                   jax.ShapeDtypeStruct((N_TOK, KV_WIDTH), bf16),
                   jax.ShapeDtypeStruct((N_TOK, KV_WIDTH), bf16)],
        scratch_shapes=[pltpu.VMEM((TB, AB_IN), f32)],
        compiler_params=pltpu.CompilerParams(dimension_semantics=("parallel",),
                                             vmem_limit_bytes=VMEM_LIMIT),
        name="ab_in_proj",
    )(xs, mods, n1, w_in, gm_g, ws, gbias, qg, kg, cosf, sinf)


def _attn_kernel(q_ref, kl_ref, kc_ref, vl_ref, vc_ref, o_ref, m_sc, l_sc, acc_sc, s_sc, p_sc):
    j = pl.program_id(0)
    scale = HEAD_DIM ** -0.5
    group = N_Q_HEADS // N_KV_HEADS
    n_chunks = SEQ // ATT_CK

    def scores(hq, k_blk):
        return lax.dot_general(q_ref[:, hq * HEAD_DIM:(hq + 1) * HEAD_DIM], k_blk, _NT,
                               preferred_element_type=f32) * scale

    for hq in range(N_Q_HEADS):
        klo = (hq // group) * HEAD_DIM
        s = scores(hq, kc_ref[:, klo:klo + HEAD_DIM])
        m = jnp.max(s, axis=-1, keepdims=True)
        p = jnp.exp(s - m)
        m_sc[hq] = m
        l_sc[hq] = jnp.sum(p, axis=-1, keepdims=True)
        acc_sc[hq] = jnp.dot(p.astype(bf16), vc_ref[:, klo:klo + HEAD_DIM], preferred_element_type=f32)

    @pl.when(j < ATT_NB_LAT)
    def _():
        for hq in range(N_Q_HEADS):
            klo = (hq // group) * HEAD_DIM
            s_sc[hq] = scores(hq, kl_ref[0:ATT_CK, klo:klo + HEAD_DIM])

        def step(i, carry):
            r0 = pl.multiple_of(i * ATT_CK, ATT_CK)
            rn = pl.multiple_of(jnp.minimum(i + 1, n_chunks - 1) * ATT_CK, ATT_CK)
            for hq in range(N_Q_HEADS):
                klo = (hq // group) * HEAD_DIM
                s = s_sc[hq]
                m_old = m_sc[hq]
                m_new = jnp.maximum(m_old, jnp.max(s, axis=-1, keepdims=True))
                alpha = jnp.exp(m_old - m_new)
                p = jnp.exp(s - m_new)
                l_sc[hq] = alpha * l_sc[hq] + jnp.sum(p, axis=-1, keepdims=True)
                m_sc[hq] = m_new
                p_sc[hq] = p.astype(bf16)
                acc_sc[hq] = alpha * acc_sc[hq] + jnp.dot(
                    p_sc[hq], vl_ref[pl.ds(r0, ATT_CK), klo:klo + HEAD_DIM], preferred_element_type=f32)
                s_sc[hq] = scores(hq, kl_ref[pl.ds(rn, ATT_CK), klo:klo + HEAD_DIM])
            return carry
        lax.fori_loop(0, n_chunks, step, 0)

    for hq in range(N_Q_HEADS):
        o_ref[:, hq * HEAD_DIM:(hq + 1) * HEAD_DIM] = (acc_sc[hq] / l_sc[hq]).astype(bf16)


def _attention(q, k, v):
    tq = ATT_TQ
    batch_of = lambda j: jnp.where(j < ATT_NB_LAT, j // (SEQ // tq), j - ATT_NB_LAT)
    lat_spec = pl.BlockSpec((SEQ, KV_WIDTH), lambda j: (batch_of(j), 0))
    ctx_spec = pl.BlockSpec((CTX_LEN, KV_WIDTH), lambda j: (N_LAT // CTX_LEN + batch_of(j), 0))
    return pl.pallas_call(
        _attn_kernel,
        grid=(N_TOK // tq,),
        in_specs=[pl.BlockSpec((tq, B_WIDTH), lambda j: (j, 0)), lat_spec, ctx_spec, lat_spec, ctx_spec],
        out_specs=pl.BlockSpec((tq, B_WIDTH), lambda j: (j, 0)),
        out_shape=jax.ShapeDtypeStruct((N_TOK, B_WIDTH), bf16),
        scratch_shapes=[pltpu.VMEM((N_Q_HEADS, tq, 1), f32), pltpu.VMEM((N_Q_HEADS, tq, 1), f32),
                        pltpu.VMEM((N_Q_HEADS, tq, HEAD_DIM), f32),
                        pltpu.VMEM((N_Q_HEADS, tq, ATT_CK), f32), pltpu.VMEM((N_Q_HEADS, tq, ATT_CK), bf16)],
        compiler_params=pltpu.CompilerParams(dimension_semantics=("parallel",),
                                             vmem_limit_bytes=VMEM_LIMIT),
        name="gqa_attention",
    )(q, k, k, v, v)


def _out_proj_kernel(*refs, n_in):
    a_refs = refs[:n_in]
    w_refs = refs[n_in:2 * n_in]
    x_ref, mod_ref, n2_ref, x1_ref, h2_ref = refs[2 * n_in:]
    y = jnp.dot(a_refs[0][...], w_refs[0][...], preferred_element_type=f32)
    for a_ref, w_ref in zip(a_refs[1:], w_refs[1:]):
        y = y + jnp.dot(a_ref[...], w_ref[...], preferred_element_type=f32)
    x1 = x_ref[...] + _mrow(mod_ref, MOD_G1) * y
    x1_ref[...] = x1
    h2 = _rms(x1, n2_ref[...]) * (1.0 + _mrow(mod_ref, MOD_SC2)) + _mrow(mod_ref, MOD_SH2)
    h2_ref[...] = h2.astype(bf16)


def _out_proj(acts, weights, xs, mods, n2, n_blocks):
    n_rows = n_blocks * TB
    in_specs = [pl.BlockSpec((TB, a.shape[1]), lambda j: (j, 0)) for a in acts]
    in_specs += [pl.BlockSpec(w.shape, lambda j: (0, 0)) for w in weights]
    in_specs += [pl.BlockSpec((TB, D_MODEL), lambda j: (j, 0)),
                 pl.BlockSpec((1, 8, D_MODEL), lambda j: (_mod_row_of_block(j), 0, 0)),
                 pl.BlockSpec((1, D_MODEL), lambda j: (0, 0))]
    return pl.pallas_call(
        functools.partial(_out_proj_kernel, n_in=len(acts)),
        grid=(n_blocks,),
        in_specs=in_specs,
        out_specs=[pl.BlockSpec((TB, D_MODEL), lambda j: (j, 0)),
                   pl.BlockSpec((TB, D_MODEL), lambda j: (j, 0))],
        out_shape=[jax.ShapeDtypeStruct((n_rows, D_MODEL), f32),
                   jax.ShapeDtypeStruct((n_rows, D_MODEL), bf16)],
        compiler_params=pltpu.CompilerParams(dimension_semantics=("parallel",),
                                             vmem_limit_bytes=VMEM_LIMIT),
        name="mixer_out_proj",
    )(*acts, *weights, xs, mods, n2)


def _oddeven_merge(lo, hi, r):
    step = r * 2
    if step < hi - lo:
        yield from _oddeven_merge(lo, hi, step)
        yield from _oddeven_merge(lo + r, hi, step)
        yield from ((i, i + r) for i in range(lo + r, hi - r, step))
    else:
        yield (lo, lo + r)


def _oddeven_merge_sort(lo, hi):
    if hi - lo >= 1:
        mid = lo + (hi - lo) // 2
        yield from _oddeven_merge_sort(lo, mid)
        yield from _oddeven_merge_sort(mid + 1, hi)
        yield from _oddeven_merge(lo, hi, 1)


_SORT_TOPK = tuple(_oddeven_merge_sort(0, PEER_TOPK - 1))


def _compare_exchange(v, i, j):
    v[i], v[j] = jnp.maximum(v[i], v[j]), jnp.minimum(v[i], v[j])


def _merge_top(x, y):
    n = PEER_TOPK
    z = list(x)
    for i in range(n - len(y), n):
        z[i] = jnp.maximum(x[i], y[n - 1 - i])
    d = n // 2
    while d >= 1:
        for i in range(n):
            if i & d == 0:
                _compare_exchange(z, i, i + d)
        d //= 2
    return z


def _pair_allowed(k, l):
    return (k + 1) * (l + 1) <= PEER_TOPK


def _route_kernel(h_ref, wqt_ref, keys_ref, c0_ref, e0_ref, r1_ref, e1_ref, q_sc, s_sc, a_sc, b_sc, z_sc):
    n = PEER_TOPK
    sub = 8
    q_sc[...] = lax.dot_general(wqt_ref[...], h_ref[...], _NT,
                                preferred_element_type=f32).astype(bf16)
    tb = ROUTE_TB

    def half_topk(hh, carry):
        r0 = pl.multiple_of(hh * N_KEYS, N_KEYS)
        s = jnp.dot(keys_ref[hh], q_sc[pl.ds(r0, N_KEYS), :], preferred_element_type=f32)
        s_sc[hh] = s
        v = [s[sub * i:sub * (i + 1)] for i in range(N_KEYS // sub)]
        for i, j in _SORT_TOPK:
            _compare_exchange(v, i, j)
        shift = sub // 2
        while shift >= 1:
            v = _merge_top(v, [pltpu.roll(t, shift, 0) for t in v])
            shift //= 2
        slot = (hh % 2) * PEER_HEADS + hh // 2
        for k in range(n):
            a_sc[k, pl.ds(slot, 1), :] = v[k][0:1, :]
        return carry

    lax.fori_loop(0, 2 * PEER_HEADS, half_topk, 0)

    a0 = [a_sc[k, 0:PEER_HEADS, :] for k in range(n)]
    a1 = [a_sc[k, PEER_HEADS:2 * PEER_HEADS, :] for k in range(n)]
    lists = [[a0[0] + a1[l] for l in range(n)],
             [a0[1] + a1[l] for l in range(n) if _pair_allowed(1, l)],
             [a0[k] + a1[0] for k in range(2, n)],
             [a0[k] + a1[1] for k in range(2, n) if _pair_allowed(k, 1)]]
    lists += [[a0[k] + a1[l] for l in range(2, n) if _pair_allowed(k, l)] for k in range(2, n)]
    top = lists[0]
    for cand in lists[1:]:
        if cand:
            top = _merge_top(top, cand)
    tau = top[n - 1]
    z = jnp.ones_like(tau)
    for k in range(1, n):
        z = z + jnp.exp(top[k] - top[0])
    z_sc[...] = z
    for l in range(n):
        bound = jnp.full((PEER_HEADS, tb), jnp.inf, f32)
        for k in range(n):
            if _pair_allowed(k, l):
                bound = jnp.where(a0[k] + a1[l] >= tau, a0[k], bound)
        b_sc[l] = bound

    def head_tables(h, carry):
        s0 = s_sc[2 * h]
        s1 = s_sc[2 * h + 1]
        count0 = jnp.zeros((N_KEYS, tb), f32)
        rank1 = jnp.zeros((N_KEYS, tb), f32)
        for k in range(n):
            count0 = count0 + jnp.where(s0 >= b_sc[k, pl.ds(h, 1), :], 1.0, 0.0)
            rank1 = rank1 + jnp.where(a_sc[k, pl.ds(PEER_HEADS + h, 1), :] > s1, 1.0, 0.0)
        inv_z = 1.0 / z_sc[pl.ds(h, 1), :]
        e0 = jnp.where(s0 >= a_sc[n - 1, pl.ds(h, 1), :], jnp.exp(s0 - a_sc[0, pl.ds(h, 1), :]), 0.0)
        e1 = jnp.where(s1 >= a_sc[n - 1, pl.ds(PEER_HEADS + h, 1), :],
                       jnp.exp(s1 - a_sc[0, pl.ds(PEER_HEADS + h, 1), :]), 0.0)
        c0_ref[h] = count0
        e0_ref[h] = e0 * inv_z
        r1_ref[h] = rank1.astype(bf16)
        e1_ref[h] = e1.astype(bf16)
        return carry

    lax.fori_loop(0, PEER_HEADS, head_tables, 0)


def _peer_route(h2, wq_t, keys):
    n_rows = h2.shape[0]
    tab = lambda dtype: jax.ShapeDtypeStruct((PEER_HEADS, N_KEYS, n_rows), dtype)
    tb = ROUTE_TB
    tab_spec = pl.BlockSpec((PEER_HEADS, N_KEYS, tb), lambda j: (0, 0, j))
    return pl.pallas_call(
        _route_kernel,
        grid=(n_rows // tb,),
        in_specs=[pl.BlockSpec((tb, D_MODEL), lambda j: (j, 0)),
                  pl.BlockSpec(wq_t.shape, lambda j: (0, 0)),
                  pl.BlockSpec(keys.shape, lambda j: (0, 0, 0))],
        out_specs=[tab_spec] * 4,
        out_shape=[tab(f32), tab(f32), tab(bf16), tab(bf16)],
        scratch_shapes=[pltpu.VMEM((2 * PEER_HEADS * N_KEYS, tb), bf16),
                        pltpu.VMEM((2 * PEER_HEADS, N_KEYS, tb), f32),
                        pltpu.VMEM((PEER_TOPK, 2 * PEER_HEADS, tb), f32),
                        pltpu.VMEM((PEER_TOPK, PEER_HEADS, tb), f32),
                        pltpu.VMEM((PEER_HEADS, tb), f32)],
        compiler_params=pltpu.CompilerParams(dimension_semantics=("parallel",),
                                             vmem_limit_bytes=VMEM_LIMIT),
        name="peer_route",
    )(h2, wq_t, keys)


def _peer_dense_kernel(h_ref, u_ref, vt_ref, c0_ref, e0_ref, r1_ref, e1_ref, x1_ref, mod_ref, fg_ref,
                       o_ref, acc_sc, sc_sc, act_sc, *, final_norm):
    g = pl.program_id(1)

    @pl.when(g == 0)
    def _():
        acc_sc[...] = jnp.zeros_like(acc_sc)
        sc_sc[...] = jnp.zeros_like(sc_sc)

    pk = 16
    zero = jnp.zeros((pk, PEER_TC), bf16)
    for tc in range(PEER_TT // PEER_TC):
        tok = slice(tc * PEER_TC, (tc + 1) * PEER_TC)
        for ii in range(PEER_EB // N_KEYS):
            c0 = [jnp.broadcast_to(c0_ref[h, ii:ii + 1, tok], (pk, PEER_TC)).astype(bf16)
                  for h in range(PEER_HEADS)]
            e0 = [jnp.broadcast_to(e0_ref[h, ii:ii + 1, tok], (pk, PEER_TC)).astype(bf16)
                  for h in range(PEER_HEADS)]
            for q in range(N_KEYS // pk):
                lo = q * pk
                gate = zero
                for h in range(PEER_HEADS):
                    picked = r1_ref[h, lo:lo + pk, tok] < c0[h]
                    gate = gate + jnp.where(picked, e1_ref[h, lo:lo + pk, tok], zero) * e0[h]
                row = ii * N_KEYS + lo
                act_sc[row:row + pk, tok] = _gelu(sc_sc[row:row + pk, tok]).astype(bf16) * gate
        acc_sc[:, tok] += jnp.dot(vt_ref[...], act_sc[:, tok], preferred_element_type=f32)
        sc_sc[:, tok] = lax.dot_general(u_ref[...], h_ref[tok, :], _NT, preferred_element_type=f32)

    @pl.when(g == pl.num_programs(1) - 1)
    def _():
        x2 = x1_ref[...] + _mrow(mod_ref, MOD_G2) * acc_sc[...].T
        if final_norm:
            x2 = _rms(x2, fg_ref[...])
        o_ref[...] = x2


def _peer_dense(h2, tabs, u_tab, v_tab_t, x1, mods, final_g, final_norm):
    n_rows = h2.shape[0]
    c0, e0, r1, e1 = tabs
    tiles_per_seq = SEQ // PEER_TT
    n_eb = N_EXPERTS // PEER_EB
    mod_row = lambda t, g: (jnp.where(t < N_LAT // PEER_TT, t // tiles_per_seq, 2), 0, 0)
    cur = lambda g: jnp.minimum(g, n_eb - 1)
    prev = lambda g: jnp.maximum(g - 1, 0)
    row_spec = pl.BlockSpec((PEER_HEADS, PEER_EB // N_KEYS, PEER_TT), lambda t, g: (0, prev(g), t))
    key_spec = pl.BlockSpec((PEER_HEADS, N_KEYS, PEER_TT), lambda t, g: (0, 0, t))
    return pl.pallas_call(
        functools.partial(_peer_dense_kernel, final_norm=final_norm),
        grid=(pl.cdiv(n_rows, PEER_TT), n_eb + 1),
        in_specs=[pl.BlockSpec((PEER_TT, D_MODEL), lambda t, g: (t, 0)),
                  pl.BlockSpec((PEER_EB, D_MODEL), lambda t, g: (cur(g), 0)),
                  pl.BlockSpec((D_MODEL, PEER_EB), lambda t, g: (0, prev(g))),
                  row_spec, row_spec, key_spec, key_spec,
                  pl.BlockSpec((PEER_TT, D_MODEL), lambda t, g: (t, 0)),
                  pl.BlockSpec((1, 8, D_MODEL), mod_row),
                  pl.BlockSpec((1, D_MODEL), lambda t, g: (0, 0))],
        out_specs=pl.BlockSpec((PEER_TT, D_MODEL), lambda t, g: (t, 0)),
        out_shape=jax.ShapeDtypeStruct((n_rows, D_MODEL), f32),
        scratch_shapes=[pltpu.VMEM((D_MODEL, PEER_TT), f32),
                        pltpu.VMEM((PEER_EB, PEER_TT), f32),
                        pltpu.VMEM((PEER_EB, PEER_TT), bf16)],
        compiler_params=pltpu.CompilerParams(dimension_semantics=("parallel", "arbitrary"),
                                             vmem_limit_bytes=VMEM_LIMIT),
        name="peer_dense",
    )(h2, u_tab, v_tab_t, c0, e0, r1, e1, x1, mods, final_g)


def _rg_in_kernel(x_ref, mod_ref, n1_ref, win_ref, gg_ref, xr_ref):
    h = _rms(x_ref[...], n1_ref[...]) * (1.0 + _mrow(mod_ref, MOD_SC1)) + _mrow(mod_ref, MOD_SH1)
    z = jnp.dot(h.astype(bf16), win_ref[...], preferred_element_type=f32)
    gg_ref[...] = _gelu(z[:, :D_RNN]).astype(bf16)
    xr_ref[...] = z[:, D_RNN:]


def _rg_in(xs, mods, n1, w_in):
    return pl.pallas_call(
        _rg_in_kernel,
        grid=(NB_ALL,),
        in_specs=[pl.BlockSpec((TB, D_MODEL), lambda j: (j, 0)),
                  pl.BlockSpec((1, 8, D_MODEL), lambda j: (_mod_row_of_block(j), 0, 0)),
                  pl.BlockSpec((1, D_MODEL), lambda j: (0, 0)),
                  pl.BlockSpec((D_MODEL, 2 * D_RNN), lambda j: (0, 0))],
        out_specs=[pl.BlockSpec((TB, D_RNN), lambda j: (j, 0)),
                   pl.BlockSpec((TB, D_RNN), lambda j: (j, 0))],
        out_shape=[jax.ShapeDtypeStruct((N_TOK, D_RNN), bf16),
                   jax.ShapeDtypeStruct((N_TOK, D_RNN), f32)],
        compiler_params=pltpu.CompilerParams(dimension_semantics=("parallel",),
                                             vmem_limit_bytes=VMEM_LIMIT),
        name="rglru_in_proj",
    )(xs, mods, n1, w_in)


def _rglru_kernel(xl_ref, xc_ref, gg_ref, cw_ref, cb_ref, wa_ref, wx_ref, ba_ref, bx_ref, lam_ref,
                  y_ref, xpad_sc, cpad_sc, hf_sc):
    R = SCAN_R
    n_tiles = SEQ // R
    pad = 8
    zeros = jnp.zeros((pad, RNN_BLOCK), f32)
    xpad_sc[0:pad] = zeros
    xpad_sc[pad + SEQ:2 * pad + SEQ] = zeros
    cpad_sc[0:pad] = zeros
    cpad_sc[pad + CTX_LEN:2 * pad + CTX_LEN] = zeros
    cpad_sc[pad:pad + CTX_LEN] = xc_ref[...]

    def copy_in(t, carry):
        r0 = pl.multiple_of(t * R, R)
        xpad_sc[pl.ds(r0 + pad, R)] = xl_ref[pl.ds(r0, R)]
        return carry
    lax.fori_loop(0, n_tiles, copy_in, 0)

    cw = cw_ref[...]
    cb = cb_ref[...]

    def conv(win):
        acc = cb
        for w in range(CONV_W):
            lo = pad - CONV_W // 2 + w
            acc = acc + win[lo:lo + R] * cw[w:w + 1]
        return acc

    def coeffs(xc, d):
        xb = xc.astype(bf16)
        r = jax.nn.sigmoid(jnp.dot(xb, wa_ref[d, 0], preferred_element_type=f32) + ba_ref[d:d + 1, :])
        i = jax.nn.sigmoid(jnp.dot(xb, wx_ref[d, 0], preferred_element_type=f32) + bx_ref[d:d + 1, :])
        nl = -lam_ref[d:d + 1, :]
        softplus = jnp.maximum(nl, 0.0) + jnp.log1p(jnp.exp(-jnp.abs(nl)))
        log_a = (-RG_C) * r * softplus
        a = jnp.exp(log_a)
        one_minus_a2 = -jnp.tanh(log_a) * (a * a + 1.0)
        return a, jnp.sqrt(one_minus_a2) * (i * xc)

    row = lax.broadcasted_iota(jnp.int32, (R, RNN_BLOCK), 0)

    def scan(a, u, reverse):
        s = 1
        while s < R:
            if reverse:
                edge = row >= R - s
                a_prev = pltpu.roll(a, R - s, 0)
                u_prev = pltpu.roll(u, R - s, 0)
            else:
                edge = row < s
                a_prev = pltpu.roll(a, s, 0)
                u_prev = pltpu.roll(u, s, 0)
            u = a * jnp.where(edge, 0.0, u_prev) + u
            a = a * jnp.where(edge, 1.0, a_prev)
            s *= 2
        return a, u

    xcc = conv(cpad_sc[...])
    _, hc = scan(*coeffs(xcc, 0), reverse=False)
    end_f = hc[R - 1:R]
    _, hc = scan(*coeffs(xcc, 1), reverse=True)
    end_b = hc[0:1]

    def fwd(t, carry):
        r0 = pl.multiple_of(t * R, R)
        a, u = coeffs(conv(xpad_sc[pl.ds(r0, R + 2 * pad)]), 0)
        a, u = scan(a, u, reverse=False)
        h = u + a * carry
        hf_sc[pl.ds(r0, R)] = h
        return h[R - 1:R]
    lax.fori_loop(0, n_tiles, fwd, end_f)

    def bwd(tt, carry):
        r0 = pl.multiple_of((n_tiles - 1 - tt) * R, R)
        a, u = coeffs(conv(xpad_sc[pl.ds(r0, R + 2 * pad)]), 1)
        a, u = scan(a, u, reverse=True)
        h = u + a * carry
        y = gg_ref[pl.ds(r0, R)].astype(f32) * (hf_sc[pl.ds(r0, R)] + h)
        y_ref[pl.ds(r0, R)] = y.astype(bf16)
        return h[0:1]
    lax.fori_loop(0, n_tiles, bwd, end_b)


def _rglru(xr, gg, conv_w, conv_b, wa, wx, ba, bx, lam):
    blk = RNN_BLOCK
    ctx_blk0 = N_LAT // CTX_LEN
    vec = lambda rows: pl.BlockSpec((rows, blk), lambda b, c: (0, c))
    wspec = pl.BlockSpec((2, 1, blk, blk), lambda b, c: (0, c, 0, 0))
    return pl.pallas_call(
        _rglru_kernel,
        grid=(BATCH, N_RNN_BLOCKS),
        in_specs=[pl.BlockSpec((SEQ, blk), lambda b, c: (b, c)),
                  pl.BlockSpec((CTX_LEN, blk), lambda b, c: (ctx_blk0 + b, c)),
                  pl.BlockSpec((SEQ, blk), lambda b, c: (b, c)),
                  vec(CONV_W), vec(1), wspec, wspec, vec(2), vec(2), vec(2)],
        out_specs=pl.BlockSpec((SEQ, blk), lambda b, c: (b, c)),
        out_shape=jax.ShapeDtypeStruct((N_LAT, D_RNN), bf16),
        scratch_shapes=[pltpu.VMEM((SEQ + 16, blk), f32),
                        pltpu.VMEM((CTX_LEN + 16, blk), f32),
                        pltpu.VMEM((SEQ, blk), f32)],
        compiler_params=pltpu.CompilerParams(dimension_semantics=("parallel", "parallel"),
                                             vmem_limit_bytes=VMEM_LIMIT),
        name="rglru_scan",
    )(xr, xr, gg, conv_w, conv_b, wa, wx, ba, bx, lam)


def _rope_tables():
    t = jnp.arange(SEQ, dtype=jnp.int32)
    n_freq = HEAD_DIM // 4
    inv_freq = ROPE_THETA ** (-jnp.arange(n_freq, dtype=f32) / n_freq)
    ang_r = (t // GRID_W).astype(f32)[:, None] * inv_freq
    ang_c = (t % GRID_W).astype(f32)[:, None] * inv_freq
    cos = jnp.concatenate([jnp.cos(ang_r)] * 2 + [jnp.cos(ang_c)] * 2, axis=1)
    sin = jnp.concatenate([-jnp.sin(ang_r), jnp.sin(ang_r), -jnp.sin(ang_c), jnp.sin(ang_c)], axis=1)
    cos = jnp.concatenate([cos, jnp.ones((TB, HEAD_DIM), f32)], axis=0)
    sin = jnp.concatenate([sin, jnp.zeros((TB, HEAD_DIM), f32)], axis=0)
    return cos, sin


def _peer(h2, x1, mods, wq, keys, u_tab, v_tab, final_g, final_norm):
    tabs = _peer_route(h2, wq.T.astype(bf16),
                       keys.reshape(2 * PEER_HEADS, N_KEYS, D_KEY // 2).astype(bf16))
    return _peer_dense(h2, tabs, u_tab.astype(bf16), v_tab.T.astype(bf16), x1, mods,
                       final_g.reshape(1, D_MODEL), final_norm)


def kernel(x, c, ctx, c_ctx, mod_w, mod_b, norm1_g, norm2_g, ab_w_in, gmlp_norm_g, gmlp_ws, gmlp_bs,
           q_norm_g, k_norm_g, ab_w_out, rg_w_in, rg_conv_w, rg_conv_b, rg_wa, rg_ba, rg_wx, rg_bx,
           rg_lambda, rg_w_out, peer_wq, peer_keys, peer_u, peer_v, final_norm_g):
    xs = jnp.concatenate([x.reshape(N_LAT, D_MODEL), ctx.reshape(BATCH * CTX_LEN, D_MODEL)], axis=0)
    cvec = jnp.concatenate([c, c_ctx[None, :], jnp.zeros((8 - BATCH - 1, D_MODEL), f32)], axis=0)
    mods = _mod_vectors(cvec, mod_w, mod_b)
    cosf, sinf = _rope_tables()
    row = lambda v: v.reshape(1, -1)

    gbias = jnp.broadcast_to(gmlp_bs[0][:, :, None], (A_GROUPS, CHUNK, CHUNK))
    ya, q, k, v = _ab_in(xs, mods[0], row(norm1_g[0]), ab_w_in[0].astype(bf16), row(gmlp_norm_g[0]),
                         gmlp_ws[0].astype(bf16), gbias, row(q_norm_g[0]), row(k_norm_g[0]), cosf, sinf)
    o = _attention(q, k, v)
    w_out = ab_w_out[0].astype(bf16)
    x1, h2 = _out_proj([ya, o], [w_out[:A_WIDTH], w_out[A_WIDTH:]], xs, mods[0], row(norm2_g[0]), NB_ALL)
    xs = _peer(h2, x1, mods[0], peer_wq[0], peer_keys[0], peer_u[0], peer_v[0], final_norm_g, False)

    gg, xr = _rg_in(xs, mods[1], row(norm1_g[1]), rg_w_in[0].astype(bf16))
    y = _rglru(xr, gg, rg_conv_w[0], row(rg_conv_b[0]), rg_wa[0].astype(bf16), rg_wx[0].astype(bf16),
               rg_ba[0], rg_bx[0], rg_lambda[0])
    x1, h2 = _out_proj([y], [rg_w_out[0].astype(bf16)], xs, mods[1], row(norm2_g[1]), NB_LAT)
    out = _peer(h2, x1, mods[1], peer_wq[1], peer_keys[1], peer_u[1], peer_v[1], final_norm_g, True)
    return out.reshape(BATCH, SEQ, D_MODEL)
```

```python
import functools

import jax
import jax.numpy as jnp
from jax import lax
from jax.experimental import pallas as pl
from jax.experimental.pallas import tpu as pltpu

f32 = jnp.float32
bf16 = jnp.bfloat16

D_MODEL = 1024
BATCH = 2
SEQ = 8192
DEPTH = 2
GRID_W = 64
CTX_LEN = 256
EPS = 1e-6
N_MOD = 6
A_GROUPS = 4
A_WIDTH = 512
CHUNK = 128
N_Q_HEADS = 4
N_KV_HEADS = 2
HEAD_DIM = 128
ROPE_THETA = 10000.0
B_WIDTH = N_Q_HEADS * HEAD_DIM
KV_WIDTH = N_KV_HEADS * HEAD_DIM
AB_IN = 2 * A_WIDTH + B_WIDTH + 2 * KV_WIDTH
D_RNN = 1280
RNN_BLOCK = 128
N_RNN_BLOCKS = D_RNN // RNN_BLOCK
CONV_W = 4
RG_C = 8.0
PEER_HEADS = 8
N_KEYS = 128
N_EXPERTS = N_KEYS * N_KEYS
PEER_TOPK = 16
D_KEY = 256

N_LAT = BATCH * SEQ
N_TOK = N_LAT + BATCH * CTX_LEN
TB = 256
NB_LAT = N_LAT // TB
NB_ALL = N_TOK // TB
NB_SEQ = SEQ // TB
ATT_TQ = CTX_LEN
ATT_NB_LAT = N_LAT // ATT_TQ
ATT_CK = 1024
ATT_QSCALE = HEAD_DIM ** -0.5 * 1.4426950408889634
ROUTE_TB = 512
PEER_TT = 512
PEER_TC = 256
PEER_EB = 1024
SCAN_R = 256
VMEM_LIMIT = 56 * 1024 * 1024

MOD_SH1, MOD_SC1, MOD_G1, MOD_SH2, MOD_SC2, MOD_G2 = range(6)

_NT = (((1,), (1,)), ((), ()))


def _rms(x, g):
    return x * lax.rsqrt(jnp.mean(x * x, axis=-1, keepdims=True) + EPS) * g


def _gelu(x):
    hx = 0.5 * x
    return hx + hx * jnp.tanh(x * (0.7978845608028654 + 0.035677408136300125 * (x * x)))


def _mod_row_of_block(j):
    return jnp.where(j < NB_LAT, j // NB_SEQ, 2)


def _mrow(mod_ref, k):
    return mod_ref[0, k:k + 1, :]


def _mod_kernel(c_ref, w_ref, b_ref, o_ref):
    c = c_ref[...]
    s = c * jax.nn.sigmoid(c)
    o_ref[0] = jnp.dot(s, w_ref[0], precision=lax.Precision.HIGHEST,
                       preferred_element_type=f32) + b_ref[0]


def _mod_vectors(cvec, mod_w, mod_b):
    nb = 1536
    width = N_MOD * D_MODEL
    out = pl.pallas_call(
        _mod_kernel,
        grid=(DEPTH, width // nb),
        in_specs=[pl.BlockSpec((8, D_MODEL), lambda l, n: (0, 0)),
                  pl.BlockSpec((1, D_MODEL, nb), lambda l, n: (l, 0, n)),
                  pl.BlockSpec((1, 1, nb), lambda l, n: (l, 0, n))],
        out_specs=pl.BlockSpec((1, 8, nb), lambda l, n: (l, 0, n)),
        out_shape=jax.ShapeDtypeStruct((DEPTH, 8, width), f32),
        name="mod_vectors",
    )(cvec, mod_w, mod_b.reshape(DEPTH, 1, width))
    mods = out.reshape(DEPTH, 8, N_MOD, D_MODEL)[:, :3]
    return jnp.pad(mods, ((0, 0), (0, 0), (0, 8 - N_MOD), (0, 0)))


def _ab_in_kernel(x_ref, mod_ref, n1_ref, win_ref, gmg_ref, ws_ref, gb_ref, qg_ref, kg_ref,
                  cos_ref, sin_ref, ya_ref, q_ref, k_ref, v_ref, z_sc):
    h = _rms(x_ref[...], n1_ref[...]) * (1.0 + _mrow(mod_ref, MOD_SC1)) + _mrow(mod_ref, MOD_SH1)
    z_sc[...] = jnp.dot(h.astype(bf16), win_ref[...], preferred_element_type=f32)

    cosf = cos_ref[...]
    sinf = sin_ref[...]
    lane = lax.broadcasted_iota(jnp.int32, (TB, HEAD_DIM), 1)
    low_part = (lane & 32) == 0

    def rope(t):
        partner = jnp.where(low_part, pltpu.roll(t, HEAD_DIM - 32, 1), pltpu.roll(t, 32, 1))
        return t * cosf + partner * sinf

    for g in range(A_GROUPS):
        lo = g * CHUNK
        u = _gelu(z_sc[:, lo:lo + CHUNK])
        vg = _gelu(z_sc[:, A_WIDTH + lo:A_WIDTH + lo + CHUNK])
        vn = _rms(vg, gmg_ref[:, lo:lo + CHUNK]).astype(bf16)
        for c in range(TB // CHUNK):
            r0 = c * CHUNK
            s = jnp.dot(ws_ref[g], vn[r0:r0 + CHUNK], preferred_element_type=f32) + gb_ref[g]
            ya_ref[r0:r0 + CHUNK, lo:lo + CHUNK] = (u[r0:r0 + CHUNK] * s).astype(bf16)

    q0 = 2 * A_WIDTH
    for hq in range(N_Q_HEADS):
        lo = hq * HEAD_DIM
        qh = _rms(z_sc[:, q0 + lo:q0 + lo + HEAD_DIM], qg_ref[...])
        q_ref[:, lo:lo + HEAD_DIM] = (rope(qh) * ATT_QSCALE).astype(bf16)
    k0 = q0 + B_WIDTH
    for hk in range(N_KV_HEADS):
        lo = hk * HEAD_DIM
        kh = _rms(z_sc[:, k0 + lo:k0 + lo + HEAD_DIM], kg_ref[...])
        k_ref[:, lo:lo + HEAD_DIM] = rope(kh).astype(bf16)
    v0 = k0 + KV_WIDTH
    v_ref[...] = z_sc[:, v0:v0 + KV_WIDTH].astype(bf16)


def _ab_in(xs, mods, n1, w_in, gm_g, ws, gbias, qg, kg, cosf, sinf):
    full = lambda shape: pl.BlockSpec(shape, lambda j: (0,) * len(shape))
    rope_blk = lambda j: (jnp.where(j < NB_LAT, j % NB_SEQ, NB_SEQ), 0)
    return pl.pallas_call(
        _ab_in_kernel,
        grid=(NB_ALL,),
        in_specs=[pl.BlockSpec((TB, D_MODEL), lambda j: (j, 0)),
                  pl.BlockSpec((1, 8, D_MODEL), lambda j: (_mod_row_of_block(j), 0, 0)),
                  full((1, D_MODEL)), full((D_MODEL, AB_IN)), full((1, A_WIDTH)),
                  full((A_GROUPS, CHUNK, CHUNK)), full((A_GROUPS, CHUNK, CHUNK)),
                  full((1, HEAD_DIM)), full((1, HEAD_DIM)),
                  pl.BlockSpec((TB, HEAD_DIM), rope_blk), pl.BlockSpec((TB, HEAD_DIM), rope_blk)],
        out_specs=[pl.BlockSpec((TB, A_WIDTH), lambda j: (j, 0)),
                   pl.BlockSpec((TB, B_WIDTH), lambda j: (j, 0)),
                   pl.BlockSpec((TB, KV_WIDTH), lambda j: (j, 0)),
                   pl.BlockSpec((TB, KV_WIDTH), lambda j: (j, 0))],
        out_shape=[jax.ShapeDtypeStruct((N_TOK, A_WIDTH), bf16),
                   jax.ShapeDtypeStruct((N_TOK, B_WIDTH), bf16),
                   jax.ShapeDtypeStruct((N_TOK, KV_WIDTH), bf16),
                   jax.ShapeDtypeStruct((N_TOK, KV_WIDTH), bf16)],
        scratch_shapes=[pltpu.VMEM((TB, AB_IN), f32)],
        compiler_params=pltpu.CompilerParams(dimension_semantics=("parallel",),
                                             vmem_limit_bytes=VMEM_LIMIT),
        name="ab_in_proj",
    )(xs, mods, n1, w_in, gm_g, ws, gbias, qg, kg, cosf, sinf)


def _attn_kernel(q_ref, kl_ref, kc_ref, vl_ref, vc_ref, o_ref, m_sc, l_sc, acc_sc, s_sc, p_sc):
    j = pl.program_id(0)
    group = N_Q_HEADS // N_KV_HEADS
    n_chunks = SEQ // ATT_CK

    def scores(hq, k_blk):
        return lax.dot_general(q_ref[:, hq * HEAD_DIM:(hq + 1) * HEAD_DIM], k_blk, _NT,
                               preferred_element_type=f32)

    for hq in range(N_Q_HEADS):
        klo = (hq // group) * HEAD_DIM
        s = scores(hq, kc_ref[:, klo:klo + HEAD_DIM])
        m = jnp.max(s, axis=-1, keepdims=True)
        p = jnp.exp2(s - m)
        m_sc[hq] = m
        l_sc[hq] = jnp.sum(p, axis=-1, keepdims=True)
        acc_sc[hq] = jnp.dot(p.astype(bf16), vc_ref[:, klo:klo + HEAD_DIM], preferred_element_type=f32)

    @pl.when(j < ATT_NB_LAT)
    def _():
        for hq in range(N_Q_HEADS):
            klo = (hq // group) * HEAD_DIM
            s_sc[hq] = scores(hq, kl_ref[0:ATT_CK, klo:klo + HEAD_DIM])

        def step(i, carry):
            r0 = pl.multiple_of(i * ATT_CK, ATT_CK)
            rn = pl.multiple_of(jnp.minimum(i + 1, n_chunks - 1) * ATT_CK, ATT_CK)
            for hq in range(N_Q_HEADS):
                klo = (hq // group) * HEAD_DIM
                s = s_sc[hq]
                m_old = m_sc[hq]
                m_new = jnp.maximum(m_old, jnp.max(s, axis=-1, keepdims=True))
                alpha = jnp.exp2(m_old - m_new)
                p = jnp.exp2(s - m_new)
                l_sc[hq] = alpha * l_sc[hq] + jnp.sum(p, axis=-1, keepdims=True)
                m_sc[hq] = m_new
                p_sc[hq] = p.astype(bf16)
                acc_sc[hq] = alpha * acc_sc[hq] + jnp.dot(
                    p_sc[hq], vl_ref[pl.ds(r0, ATT_CK), klo:klo + HEAD_DIM], preferred_element_type=f32)
                s_sc[hq] = scores(hq, kl_ref[pl.ds(rn, ATT_CK), klo:klo + HEAD_DIM])
            return carry
        lax.fori_loop(0, n_chunks, step, 0)

    for hq in range(N_Q_HEADS):
        o_ref[:, hq * HEAD_DIM:(hq + 1) * HEAD_DIM] = (acc_sc[hq] / l_sc[hq]).astype(bf16)


def _attention(q, k, v):
    tq = ATT_TQ
    batch_of = lambda j: jnp.where(j < ATT_NB_LAT, j // (SEQ // tq), j - ATT_NB_LAT)
    lat_spec = pl.BlockSpec((SEQ, KV_WIDTH), lambda j: (batch_of(j), 0))
    ctx_spec = pl.BlockSpec((CTX_LEN, KV_WIDTH), lambda j: (N_LAT // CTX_LEN + batch_of(j), 0))
    return pl.pallas_call(
        _attn_kernel,
        grid=(N_TOK // tq,),
        in_specs=[pl.BlockSpec((tq, B_WIDTH), lambda j: (j, 0)), lat_spec, ctx_spec, lat_spec, ctx_spec],
        out_specs=pl.BlockSpec((tq, B_WIDTH), lambda j: (j, 0)),
        out_shape=jax.ShapeDtypeStruct((N_TOK, B_WIDTH), bf16),
        scratch_shapes=[pltpu.VMEM((N_Q_HEADS, tq, 1), f32), pltpu.VMEM((N_Q_HEADS, tq, 1), f32),
                        pltpu.VMEM((N_Q_HEADS, tq, HEAD_DIM), f32),
                        pltpu.VMEM((N_Q_HEADS, tq, ATT_CK), f32), pltpu.VMEM((N_Q_HEADS, tq, ATT_CK), bf16)],
        compiler_params=pltpu.CompilerParams(dimension_semantics=("parallel",),
                                             vmem_limit_bytes=VMEM_LIMIT),
        name="gqa_attention",
    )(q, k, k, v, v)


def _out_proj_kernel(*refs, n_in):
    a_refs = refs[:n_in]
    w_refs = refs[n_in:2 * n_in]
    x_ref, mod_ref, n2_ref, x1_ref, h2_ref = refs[2 * n_in:]
    y = jnp.dot(a_refs[0][...], w_refs[0][...], preferred_element_type=f32)
    for a_ref, w_ref in zip(a_refs[1:], w_refs[1:]):
        y = y + jnp.dot(a_ref[...], w_ref[...], preferred_element_type=f32)
    x1 = x_ref[...] + _mrow(mod_ref, MOD_G1) * y
    x1_ref[...] = x1
    h2 = _rms(x1, n2_ref[...]) * (1.0 + _mrow(mod_ref, MOD_SC2)) + _mrow(mod_ref, MOD_SH2)
    h2_ref[...] = h2.astype(bf16)


def _out_proj(acts, weights, xs, mods, n2, n_blocks):
    n_rows = n_blocks * TB
    in_specs = [pl.BlockSpec((TB, a.shape[1]), lambda j: (j, 0)) for a in acts]
    in_specs += [pl.BlockSpec(w.shape, lambda j: (0, 0)) for w in weights]
    in_specs += [pl.BlockSpec((TB, D_MODEL), lambda j: (j, 0)),
                 pl.BlockSpec((1, 8, D_MODEL), lambda j: (_mod_row_of_block(j), 0, 0)),
                 pl.BlockSpec((1, D_MODEL), lambda j: (0, 0))]
    return pl.pallas_call(
        functools.partial(_out_proj_kernel, n_in=len(acts)),
        grid=(n_blocks,),
        in_specs=in_specs,
        out_specs=[pl.BlockSpec((TB, D_MODEL), lambda j: (j, 0)),
                   pl.BlockSpec((TB, D_MODEL), lambda j: (j, 0))],
        out_shape=[jax.ShapeDtypeStruct((n_rows, D_MODEL), f32),
                   jax.ShapeDtypeStruct((n_rows, D_MODEL), bf16)],
        compiler_params=pltpu.CompilerParams(dimension_semantics=("parallel",),
                                             vmem_limit_bytes=VMEM_LIMIT),
        name="mixer_out_proj",
    )(*acts, *weights, xs, mods, n2)


def _oddeven_merge(lo, hi, r):
    step = r * 2
    if step < hi - lo:
        yield from _oddeven_merge(lo, hi, step)
        yield from _oddeven_merge(lo + r, hi, step)
        yield from ((i, i + r) for i in range(lo + r, hi - r, step))
    else:
        yield (lo, lo + r)


def _oddeven_merge_sort(lo, hi):
    if hi - lo >= 1:
        mid = lo + (hi - lo) // 2
        yield from _oddeven_merge_sort(lo, mid)
        yield from _oddeven_merge_sort(mid + 1, hi)
        yield from _oddeven_merge(lo, hi, 1)


_SORT_TOPK = tuple(_oddeven_merge_sort(0, PEER_TOPK - 1))


def _compare_exchange(v, i, j):
    v[i], v[j] = jnp.maximum(v[i], v[j]), jnp.minimum(v[i], v[j])


def _merge_top(x, y):
    n = PEER_TOPK
    z = list(x)
    for i in range(n - len(y), n):
        z[i] = jnp.maximum(x[i], y[n - 1 - i])
    d = n // 2
    while d >= 1:
        for i in range(n):
            if i & d == 0:
                _compare_exchange(z, i, i + d)
        d //= 2
    return z


def _pair_allowed(k, l):
    return (k + 1) * (l + 1) <= PEER_TOPK


def _route_kernel(h_ref, wqt_ref, keys_ref, c0_ref, e0_ref, r1_ref, e1_ref, q_sc, s_sc, a_sc, b_sc, z_sc):
    n = PEER_TOPK
    sub = 8
    q_sc[...] = lax.dot_general(wqt_ref[...], h_ref[...], _NT,
                                preferred_element_type=f32).astype(bf16)
    tb = ROUTE_TB

    def half_topk(hh, carry):
        r0 = pl.multiple_of(hh * N_KEYS, N_KEYS)
        s = jnp.dot(keys_ref[hh], q_sc[pl.ds(r0, N_KEYS), :], preferred_element_type=f32)
        s_sc[hh] = s
        v = [s[sub * i:sub * (i + 1)] for i in range(N_KEYS // sub)]
        for i, j in _SORT_TOPK:
            _compare_exchange(v, i, j)
        shift = sub // 2
        while shift >= 1:
            v = _merge_top(v, [pltpu.roll(t, shift, 0) for t in v])
            shift //= 2
        slot = (hh % 2) * PEER_HEADS + hh // 2
        for k in range(n):
            a_sc[k, pl.ds(slot, 1), :] = v[k][0:1, :]
        return carry

    lax.fori_loop(0, 2 * PEER_HEADS, half_topk, 0)

    a0 = [a_sc[k, 0:PEER_HEADS, :] for k in range(n)]
    a1 = [a_sc[k, PEER_HEADS:2 * PEER_HEADS, :] for k in range(n)]
    lists = [[a0[0] + a1[l] for l in range(n)],
             [a0[1] + a1[l] for l in range(n) if _pair_allowed(1, l)],
             [a0[k] + a1[0] for k in range(2, n)],
             [a0[k] + a1[1] for k in range(2, n) if _pair_allowed(k, 1)]]
    lists += [[a0[k] + a1[l] for l in range(2, n) if _pair_allowed(k, l)] for k in range(2, n)]
    top = lists[0]
    for cand in lists[1:]:
        if cand:
            top = _merge_top(top, cand)
    tau = top[n - 1]
    z = jnp.ones_like(tau)
    for k in range(1, n):
        z = z + jnp.exp(top[k] - top[0])
    z_sc[...] = z
    for l in range(n):
        bound = jnp.full((PEER_HEADS, tb), jnp.inf, f32)
        for k in range(n):
            if _pair_allowed(k, l):
                bound = jnp.where(a0[k] + a1[l] >= tau, a0[k], bound)
        b_sc[l] = bound

    def head_tables(h, carry):
        s0 = s_sc[2 * h]
        s1 = s_sc[2 * h + 1]
        count0 = jnp.zeros((N_KEYS, tb), f32)
        rank1 = jnp.zeros((N_KEYS, tb), f32)
        for k in range(n):
            count0 = count0 + jnp.where(s0 >= b_sc[k, pl.ds(h, 1), :], 1.0, 0.0)
            rank1 = rank1 + jnp.where(a_sc[k, pl.ds(PEER_HEADS + h, 1), :] > s1, 1.0, 0.0)
        inv_z = 1.0 / z_sc[pl.ds(h, 1), :]
        e0 = jnp.where(s0 >= a_sc[n - 1, pl.ds(h, 1), :], jnp.exp(s0 - a_sc[0, pl.ds(h, 1), :]), 0.0)
        e1 = jnp.where(s1 >= a_sc[n - 1, pl.ds(PEER_HEADS + h, 1), :],
                       jnp.exp(s1 - a_sc[0, pl.ds(PEER_HEADS + h, 1), :]), 0.0)
        c0_ref[h] = count0
        e0_ref[h] = e0 * inv_z
        r1_ref[h] = rank1.astype(bf16)
        e1_ref[h] = e1.astype(bf16)
        return carry

    lax.fori_loop(0, PEER_HEADS, head_tables, 0)


def _peer_route(h2, wq_t, keys):
    n_rows = h2.shape[0]
    tab = lambda dtype: jax.ShapeDtypeStruct((PEER_HEADS, N_KEYS, n_rows), dtype)
    tb = ROUTE_TB
    tab_spec = pl.BlockSpec((PEER_HEADS, N_KEYS, tb), lambda j: (0, 0, j))
    return pl.pallas_call(
        _route_kernel,
        grid=(n_rows // tb,),
        in_specs=[pl.BlockSpec((tb, D_MODEL), lambda j: (j, 0)),
                  pl.BlockSpec(wq_t.shape, lambda j: (0, 0)),
                  pl.BlockSpec(keys.shape, lambda j: (0, 0, 0))],
        out_specs=[tab_spec] * 4,
        out_shape=[tab(f32), tab(f32), tab(bf16), tab(bf16)],
        scratch_shapes=[pltpu.VMEM((2 * PEER_HEADS * N_KEYS, tb), bf16),
                        pltpu.VMEM((2 * PEER_HEADS, N_KEYS, tb), f32),
                        pltpu.VMEM((PEER_TOPK, 2 * PEER_HEADS, tb), f32),
                        pltpu.VMEM((PEER_TOPK, PEER_HEADS, tb), f32),
                        pltpu.VMEM((PEER_HEADS, tb), f32)],
        compiler_params=pltpu.CompilerParams(dimension_semantics=("parallel",),
                                             vmem_limit_bytes=VMEM_LIMIT),
        name="peer_route",
    )(h2, wq_t, keys)


def _peer_dense_kernel(h_ref, u_ref, vt_ref, c0_ref, e0_ref, r1_ref, e1_ref, x1_ref, mod_ref, fg_ref,
                       o_ref, acc_sc, sc_sc, act_sc, *, final_norm):
    g = pl.program_id(1)

    @pl.when(g == 0)
    def _():
        acc_sc[...] = jnp.zeros_like(acc_sc)
        sc_sc[...] = jnp.zeros_like(sc_sc)

    pk = 16
    zero = jnp.zeros((pk, PEER_TC), bf16)
    for tc in range(PEER_TT // PEER_TC):
        tok = slice(tc * PEER_TC, (tc + 1) * PEER_TC)
        for ii in range(PEER_EB // N_KEYS):
            c0 = [jnp.broadcast_to(c0_ref[h, ii:ii + 1, tok], (pk, PEER_TC)).astype(bf16)
                  for h in range(PEER_HEADS)]
            e0 = [jnp.broadcast_to(e0_ref[h, ii:ii + 1, tok], (pk, PEER_TC)).astype(bf16)
                  for h in range(PEER_HEADS)]
            for q in range(N_KEYS // pk):
                lo = q * pk
                gate = zero
                for h in range(PEER_HEADS):
                    picked = r1_ref[h, lo:lo + pk, tok] < c0[h]
                    gate = gate + jnp.where(picked, e1_ref[h, lo:lo + pk, tok], zero) * e0[h]
                row = ii * N_KEYS + lo
                act_sc[row:row + pk, tok] = _gelu(sc_sc[row:row + pk, tok].astype(bf16)) * gate
        acc_sc[:, tok] += jnp.dot(vt_ref[...], act_sc[:, tok], preferred_element_type=f32)
        sc_sc[:, tok] = lax.dot_general(u_ref[...], h_ref[tok, :], _NT, preferred_element_type=f32)

    @pl.when(g == pl.num_programs(1) - 1)
    def _():
        x2 = x1_ref[...] + _mrow(mod_ref, MOD_G2) * acc_sc[...].T
        if final_norm:
            x2 = _rms(x2, fg_ref[...])
        o_ref[...] = x2


def _peer_dense(h2, tabs, u_tab, v_tab_t, x1, mods, final_g, final_norm):
    n_rows = h2.shape[0]
    c0, e0, r1, e1 = tabs
    tiles_per_seq = SEQ // PEER_TT
    n_eb = N_EXPERTS // PEER_EB
    mod_row = lambda t, g: (jnp.where(t < N_LAT // PEER_TT, t // tiles_per_seq, 2), 0, 0)
    cur = lambda g: jnp.minimum(g, n_eb - 1)
    prev = lambda g: jnp.maximum(g - 1, 0)
    row_spec = pl.BlockSpec((PEER_HEADS, PEER_EB // N_KEYS, PEER_TT), lambda t, g: (0, prev(g), t))
    key_spec = pl.BlockSpec((PEER_HEADS, N_KEYS, PEER_TT), lambda t, g: (0, 0, t))
    return pl.pallas_call(
        functools.partial(_peer_dense_kernel, final_norm=final_norm),
        grid=(n_rows // PEER_TT, n_eb + 1),
        in_specs=[pl.BlockSpec((PEER_TT, D_MODEL), lambda t, g: (t, 0)),
                  pl.BlockSpec((PEER_EB, D_MODEL), lambda t, g: (cur(g), 0)),
                  pl.BlockSpec((D_MODEL, PEER_EB), lambda t, g: (0, prev(g))),
                  row_spec, row_spec, key_spec, key_spec,
                  pl.BlockSpec((PEER_TT, D_MODEL), lambda t, g: (t, 0)),
                  pl.BlockSpec((1, 8, D_MODEL), mod_row),
                  pl.BlockSpec((1, D_MODEL), lambda t, g: (0, 0))],
        out_specs=pl.BlockSpec((PEER_TT, D_MODEL), lambda t, g: (t, 0)),
        out_shape=jax.ShapeDtypeStruct((n_rows, D_MODEL), f32),
        scratch_shapes=[pltpu.VMEM((D_MODEL, PEER_TT), f32),
                        pltpu.VMEM((PEER_EB, PEER_TT), f32),
                        pltpu.VMEM((PEER_EB, PEER_TT), bf16)],
        compiler_params=pltpu.CompilerParams(dimension_semantics=("parallel", "arbitrary"),
                                             vmem_limit_bytes=VMEM_LIMIT),
        name="peer_dense",
    )(h2, u_tab, v_tab_t, c0, e0, r1, e1, x1, mods, final_g)


def _rg_in_kernel(x_ref, mod_ref, n1_ref, win_ref, gg_ref, xr_ref):
    h = _rms(x_ref[...], n1_ref[...]) * (1.0 + _mrow(mod_ref, MOD_SC1)) + _mrow(mod_ref, MOD_SH1)
    z = jnp.dot(h.astype(bf16), win_ref[...], preferred_element_type=f32)
    gg_ref[...] = _gelu(z[:, :D_RNN]).astype(bf16)
    xr_ref[...] = z[:, D_RNN:]


def _rg_in(xs, mods, n1, w_in):
    return pl.pallas_call(
        _rg_in_kernel,
        grid=(NB_ALL,),
        in_specs=[pl.BlockSpec((TB, D_MODEL), lambda j: (j, 0)),
                  pl.BlockSpec((1, 8, D_MODEL), lambda j: (_mod_row_of_block(j), 0, 0)),
                  pl.BlockSpec((1, D_MODEL), lambda j: (0, 0)),
                  pl.BlockSpec((D_MODEL, 2 * D_RNN), lambda j: (0, 0))],
        out_specs=[pl.BlockSpec((TB, D_RNN), lambda j: (j, 0)),
                   pl.BlockSpec((TB, D_RNN), lambda j: (j, 0))],
        out_shape=[jax.ShapeDtypeStruct((N_TOK, D_RNN), bf16),
                   jax.ShapeDtypeStruct((N_TOK, D_RNN), f32)],
        compiler_params=pltpu.CompilerParams(dimension_semantics=("parallel",),
                                             vmem_limit_bytes=VMEM_LIMIT),
        name="rglru_in_proj",
    )(xs, mods, n1, w_in)


def _rglru_kernel(xl_ref, xc_ref, gg_ref, cw_ref, cb_ref, wa_ref, wx_ref, ba_ref, bx_ref, lam_ref,
                  y_ref, xpad_sc, cpad_sc, hf_sc):
    R = SCAN_R
    n_tiles = SEQ // R
    pad = 8
    zeros = jnp.zeros((pad, RNN_BLOCK), f32)
    xpad_sc[0:pad] = zeros
    xpad_sc[pad + SEQ:2 * pad + SEQ] = zeros
    cpad_sc[0:pad] = zeros
    cpad_sc[pad + CTX_LEN:2 * pad + CTX_LEN] = zeros
    cpad_sc[pad:pad + CTX_LEN] = xc_ref[...]

    def copy_in(t, carry):
        r0 = pl.multiple_of(t * R, R)
        xpad_sc[pl.ds(r0 + pad, R)] = xl_ref[pl.ds(r0, R)]
        return carry
    lax.fori_loop(0, n_tiles, copy_in, 0)

    cw = cw_ref[...]
    cb = cb_ref[...]

    def conv(win):
        acc = cb
        for w in range(CONV_W):
            lo = pad - CONV_W // 2 + w
            acc = acc + win[lo:lo + R] * cw[w:w + 1]
        return acc

    def coeffs(xc, d):
        xb = xc.astype(bf16)
        r = jax.nn.sigmoid(jnp.dot(xb, wa_ref[d, 0], preferred_element_type=f32) + ba_ref[d:d + 1, :])
        i = jax.nn.sigmoid(jnp.dot(xb, wx_ref[d, 0], preferred_element_type=f32) + bx_ref[d:d + 1, :])
        nl = -lam_ref[d:d + 1, :]
        softplus = jnp.maximum(nl, 0.0) + jnp.log1p(jnp.exp(-jnp.abs(nl)))
        log_a = (-RG_C) * r * softplus
        a = jnp.exp(log_a)
        one_minus_a2 = -jnp.tanh(log_a) * (a * a + 1.0)
        return a, jnp.sqrt(one_minus_a2) * (i * xc)

    row = lax.broadcasted_iota(jnp.int32, (R, RNN_BLOCK), 0)

    def scan(a, u, reverse):
        s = 1
        while s < R:
            if reverse:
                edge = row >= R - s
                a_prev = pltpu.roll(a, R - s, 0)
                u_prev = pltpu.roll(u, R - s, 0)
            else:
                edge = row < s
                a_prev = pltpu.roll(a, s, 0)
                u_prev = pltpu.roll(u, s, 0)
            u = a * jnp.where(edge, 0.0, u_prev) + u
            a = a * jnp.where(edge, 1.0, a_prev)
            s *= 2
        return a, u

    xcc = conv(cpad_sc[...])
    _, hc = scan(*coeffs(xcc, 0), reverse=False)
    end_f = hc[R - 1:R]
    _, hc = scan(*coeffs(xcc, 1), reverse=True)
    end_b = hc[0:1]

    def fwd(t, carry):
        r0 = pl.multiple_of(t * R, R)
        a, u = coeffs(conv(xpad_sc[pl.ds(r0, R + 2 * pad)]), 0)
        a, u = scan(a, u, reverse=False)
        h = u + a * carry
        hf_sc[pl.ds(r0, R)] = h
        return h[R - 1:R]
    lax.fori_loop(0, n_tiles, fwd, end_f)

    def bwd(tt, carry):
        r0 = pl.multiple_of((n_tiles - 1 - tt) * R, R)
        a, u = coeffs(conv(xpad_sc[pl.ds(r0, R + 2 * pad)]), 1)
        a, u = scan(a, u, reverse=True)
        h = u + a * carry
        y = gg_ref[pl.ds(r0, R)].astype(f32) * (hf_sc[pl.ds(r0, R)] + h)
        y_ref[pl.ds(r0, R)] = y.astype(bf16)
        return h[0:1]
    lax.fori_loop(0, n_tiles, bwd, end_b)


def _rglru(xr, gg, conv_w, conv_b, wa, wx, ba, bx, lam):
    blk = RNN_BLOCK
    ctx_blk0 = N_LAT // CTX_LEN
    vec = lambda rows: pl.BlockSpec((rows, blk), lambda b, c: (0, c))
    wspec = pl.BlockSpec((2, 1, blk, blk), lambda b, c: (0, c, 0, 0))
    return pl.pallas_call(
        _rglru_kernel,
        grid=(BATCH, N_RNN_BLOCKS),
        in_specs=[pl.BlockSpec((SEQ, blk), lambda b, c: (b, c)),
                  pl.BlockSpec((CTX_LEN, blk), lambda b, c: (ctx_blk0 + b, c)),
                  pl.BlockSpec((SEQ, blk), lambda b, c: (b, c)),
                  vec(CONV_W), vec(1), wspec, wspec, vec(2), vec(2), vec(2)],
        out_specs=pl.BlockSpec((SEQ, blk), lambda b, c: (b, c)),
        out_shape=jax.ShapeDtypeStruct((N_LAT, D_RNN), bf16),
        scratch_shapes=[pltpu.VMEM((SEQ + 16, blk), f32),
                        pltpu.VMEM((CTX_LEN + 16, blk), f32),
                        pltpu.VMEM((SEQ, blk), f32)],
        compiler_params=pltpu.CompilerParams(dimension_semantics=("parallel", "parallel"),
                                             vmem_limit_bytes=VMEM_LIMIT),
        name="rglru_scan",
    )(xr, xr, gg, conv_w, conv_b, wa, wx, ba, bx, lam)


def _rope_tables():
    t = jnp.arange(SEQ, dtype=jnp.int32)
    n_freq = HEAD_DIM // 4
    inv_freq = ROPE_THETA ** (-jnp.arange(n_freq, dtype=f32) / n_freq)
    ang_r = (t // GRID_W).astype(f32)[:, None] * inv_freq
    ang_c = (t % GRID_W).astype(f32)[:, None] * inv_freq
    cos = jnp.concatenate([jnp.cos(ang_r)] * 2 + [jnp.cos(ang_c)] * 2, axis=1)
    sin = jnp.concatenate([-jnp.sin(ang_r), jnp.sin(ang_r), -jnp.sin(ang_c), jnp.sin(ang_c)], axis=1)
    cos = jnp.concatenate([cos, jnp.ones((TB, HEAD_DIM), f32)], axis=0)
    sin = jnp.concatenate([sin, jnp.zeros((TB, HEAD_DIM), f32)], axis=0)
    return cos, sin


def _peer(h2, x1, mods, wq, keys, u_tab, v_tab, final_g, final_norm):
    tabs = _peer_route(h2, wq.T.astype(bf16),
                       keys.reshape(2 * PEER_HEADS, N_KEYS, D_KEY // 2).astype(bf16))
    return _peer_dense(h2, tabs, u_tab.astype(bf16), v_tab.T.astype(bf16), x1, mods,
                       final_g.reshape(1, D_MODEL), final_norm)


def kernel(x, c, ctx, c_ctx, mod_w, mod_b, norm1_g, norm2_g, ab_w_in, gmlp_norm_g, gmlp_ws, gmlp_bs,
           q_norm_g, k_norm_g, ab_w_out, rg_w_in, rg_conv_w, rg_conv_b, rg_wa, rg_ba, rg_wx, rg_bx,
           rg_lambda, rg_w_out, peer_wq, peer_keys, peer_u, peer_v, final_norm_g):
    xs = jnp.concatenate([x.reshape(N_LAT, D_MODEL), ctx.reshape(BATCH * CTX_LEN, D_MODEL)], axis=0)
    cvec = jnp.concatenate([c, c_ctx[None, :], jnp.zeros((8 - BATCH - 1, D_MODEL), f32)], axis=0)
    mods = _mod_vectors(cvec, mod_w, mod_b)
    cosf, sinf = _rope_tables()
    row = lambda v: v.reshape(1, -1)

    gbias = jnp.broadcast_to(gmlp_bs[0][:, :, None], (A_GROUPS, CHUNK, CHUNK))
    ya, q, k, v = _ab_in(xs, mods[0], row(norm1_g[0]), ab_w_in[0].astype(bf16), row(gmlp_norm_g[0]),
                         gmlp_ws[0].astype(bf16), gbias, row(q_norm_g[0]), row(k_norm_g[0]), cosf, sinf)
    o = _attention(q, k, v)
    w_out = ab_w_out[0].astype(bf16)
    x1, h2 = _out_proj([ya, o], [w_out[:A_WIDTH], w_out[A_WIDTH:]], xs, mods[0], row(norm2_g[0]), NB_ALL)
    xs = _peer(h2, x1, mods[0], peer_wq[0], peer_keys[0], peer_u[0], peer_v[0], final_norm_g, False)

    gg, xr = _rg_in(xs, mods[1], row(norm1_g[1]), rg_w_in[0].astype(bf16))
    y = _rglru(xr, gg, rg_conv_w[0], row(rg_conv_b[0]), rg_wa[0].astype(bf16), rg_wx[0].astype(bf16),
               rg_ba[0], rg_bx[0], rg_lambda[0])
    x1, h2 = _out_proj([y], [rg_w_out[0].astype(bf16)], xs, mods[1], row(norm2_g[1]), NB_LAT)
    out = _peer(h2, x1, mods[1], peer_wq[1], peer_keys[1], peer_u[1], peer_v[1], final_norm_g, True)
    return out.reshape(BATCH, SEQ, D_MODEL)
```
